```python
import jax, jax.numpy as jnp
from jax import lax
import numpy as np

D_MODEL = 1024
BATCH = 16
SEQ = 2048
DEPTH = 4

N_META = 16
N_A_LAYERS = DEPTH // 2
N_B_LAYERS = DEPTH - N_A_LAYERS
CONV_WIDTH = 31
HEAD_DIM = 64
N_HEADS = D_MODEL // HEAD_DIM
N_KV_HEADS = 4
Q_PER_KV = N_HEADS // N_KV_HEADS
WINDOW = 128
BLOCK = 128
D_FF = -(-8 * D_MODEL // (3 * 256)) * 256
NORM_EPS = 1e-6
NEG_INF = -1e30

kernel_name = "yoco_conformer_swa_sink_hybrid"


def rms_norm(x, g):
    xf = x.astype(jnp.float32)
    y = xf * lax.rsqrt(jnp.mean(xf * xf, axis=-1, keepdims=True) + NORM_EPS)
    return (y * g.astype(jnp.float32)).astype(x.dtype)


def layer_norm(x, g, b):
    xf = x.astype(jnp.float32)
    mu = jnp.mean(xf, axis=-1, keepdims=True)
    var = jnp.mean(jnp.square(xf - mu), axis=-1, keepdims=True)
    y = (xf - mu) * lax.rsqrt(var + NORM_EPS)
    return (y * g.astype(jnp.float32) + b.astype(jnp.float32)).astype(x.dtype)


def conformer_conv(u, w_in, b_in, dw, ln_g, ln_b, w_out, b_out):
    d = u.shape[-1]
    a = u @ w_in + b_in
    a = a[..., :d] * jax.nn.sigmoid(a[..., d:])
    a = jnp.pad(a, ((0, 0), (CONV_WIDTH - 1, 0), (0, 0)))
    c = lax.conv_general_dilated(a, dw[:, None, :], window_strides=(1,), padding='VALID',
                                 dimension_numbers=('NWC', 'WIO', 'NWC'),
                                 feature_group_count=d)
    c = jax.nn.silu(layer_norm(c, ln_g, ln_b))
    return c @ w_out + b_out


def swiglu(u, w_gate, w_up, w_down):
    return (jax.nn.silu(u @ w_gate) * (u @ w_up)) @ w_down


def shared_kv(h, g, w_kv, k_norm):
    b, l, _ = h.shape
    kv = rms_norm(h, g) @ w_kv
    k, v = jnp.split(kv, 2, axis=-1)
    k = rms_norm(k.reshape(b, l, N_KV_HEADS, HEAD_DIM), k_norm)
    v = v.reshape(b, l, N_KV_HEADS, HEAD_DIM)
    return k, v


def band_keys(t, nb):
    b = t.shape[0]
    tp = jnp.pad(t, ((0, 0), (BLOCK, 0), (0, 0), (0, 0)))
    tb = tp.reshape(b, nb + 1, BLOCK, N_KV_HEADS, HEAD_DIM)
    band = jnp.concatenate([tb[:, :-1], tb[:, 1:]], axis=2)
    meta = jnp.broadcast_to(t[:, None, :N_META], (b, nb, N_META, N_KV_HEADS, HEAD_DIM))
    return jnp.concatenate([meta, band], axis=2)


def window_mask(nb):
    qpos = jnp.arange(nb)[:, None] * BLOCK + jnp.arange(BLOCK)[None, :]
    band_pos = jnp.arange(nb)[:, None] * BLOCK - BLOCK + jnp.arange(2 * BLOCK)[None, :]
    diff = qpos[:, :, None] - band_pos[:, None, :]
    band_ok = (diff >= 0) & (diff < WINDOW) & (band_pos[:, None, :] >= N_META)
    meta_ok = jnp.arange(N_META)[None, None, :] <= qpos[:, :, None]
    return jnp.concatenate([meta_ok, band_ok], axis=-1)


def sliding_window_attention(u, w_q, q_g, sinks, w_o, k, v):
    b, l, _ = u.shape
    nb = l // BLOCK
    q = (u @ w_q).reshape(b, l, N_KV_HEADS, Q_PER_KV, HEAD_DIM)
    q = rms_norm(q, q_g).reshape(b, nb, BLOCK, N_KV_HEADS, Q_PER_KV, HEAD_DIM)
    kw = band_keys(k, nb)
    vw = band_keys(v, nb)
    s = jnp.einsum('bnqgrd,bnkgd->bgrnqk', q, kw).astype(jnp.float32) * (HEAD_DIM ** -0.5)
    s = jnp.where(window_mask(nb), s, NEG_INF)
    sink = sinks.astype(jnp.float32).reshape(1, N_KV_HEADS, Q_PER_KV, 1, 1, 1)
    m = jnp.maximum(jnp.max(s, axis=-1, keepdims=True), sink)
    p = jnp.exp(s - m)
    denom = jnp.sum(p, axis=-1, keepdims=True) + jnp.exp(sink - m)
    p = (p / denom).astype(vw.dtype)
    o = jnp.einsum('bgrnqk,bnkgd->bnqgrd', p, vw).reshape(b, l, N_HEADS * HEAD_DIM)
    return o @ w_o


def setup_inputs(seed: int = 0) -> dict:
    key = jax.random.key(seed)
    ks = jax.random.split(key, 21)
    out_scale = (2 * DEPTH) ** -0.5
    kvd = 2 * N_KV_HEADS * HEAD_DIM
    qd = N_HEADS * HEAD_DIM

    def nrm(k, shape, scale):
        return jax.random.normal(k, shape, jnp.float32) * scale

    def gain(k, shape):
        return 1.0 + nrm(k, shape, 0.02)

    return {
        "x": nrm(ks[0], (BATCH, SEQ, D_MODEL), 1.0),
        "meta_tokens": nrm(ks[1], (N_META, D_MODEL), 1.0),
        "norm_mix": gain(ks[2], (DEPTH, D_MODEL)),
        "norm_ffn": gain(ks[3], (DEPTH, D_MODEL)),
        "conv_w_in": nrm(ks[4], (N_A_LAYERS, D_MODEL, 2 * D_MODEL), D_MODEL ** -0.5),
        "conv_b_in": nrm(ks[5], (N_A_LAYERS, 2 * D_MODEL), 0.02),
        "conv_dw": nrm(ks[6], (N_A_LAYERS, CONV_WIDTH, D_MODEL), CONV_WIDTH ** -0.5),
        "conv_ln_g": gain(ks[7], (N_A_LAYERS, D_MODEL)),
        "conv_ln_b": nrm(ks[8], (N_A_LAYERS, D_MODEL), 0.02),
        "conv_w_out": nrm(ks[9], (N_A_LAYERS, D_MODEL, D_MODEL), D_MODEL ** -0.5 * out_scale),
        "conv_b_out": nrm(ks[10], (N_A_LAYERS, D_MODEL), 0.02),
        "kv_norm": gain(ks[11], (D_MODEL,)),
        "w_kv": nrm(ks[12], (D_MODEL, kvd), D_MODEL ** -0.5),
        "k_norm": gain(ks[13], (HEAD_DIM,)),
        "w_q": nrm(ks[14], (N_B_LAYERS, D_MODEL, qd), D_MODEL ** -0.5),
        "q_norm": gain(ks[15], (N_B_LAYERS, HEAD_DIM)),
        "attn_sinks": nrm(ks[16], (N_B_LAYERS, N_HEADS), 0.5),
        "w_o": nrm(ks[17], (N_B_LAYERS, qd, D_MODEL), qd ** -0.5 * out_scale),
        "ffn_w_gate": nrm(ks[18], (DEPTH, D_MODEL, D_FF), D_MODEL ** -0.5),
        "ffn_w_up": nrm(ks[19], (DEPTH, D_MODEL, D_FF), D_MODEL ** -0.5),
        "ffn_w_down": nrm(ks[20], (DEPTH, D_FF, D_MODEL), D_FF ** -0.5 * out_scale),
    }


def reference(x, meta_tokens, norm_mix, norm_ffn, conv_w_in, conv_b_in, conv_dw, conv_ln_g, conv_ln_b,
              conv_w_out, conv_b_out, kv_norm, w_kv, k_norm, w_q, q_norm, attn_sinks, w_o,
              ffn_w_gate, ffn_w_up, ffn_w_down):
    b, seq, d = x.shape
    l = N_META + seq
    lp = -(-l // BLOCK) * BLOCK
    meta = jnp.broadcast_to(meta_tokens.astype(x.dtype)[None], (b, N_META, d))
    h = jnp.concatenate([meta, x, jnp.zeros((b, lp - l, d), x.dtype)], axis=1)
    k_sh = None
    v_sh = None
    for layer in range(DEPTH):
        u = rms_norm(h, norm_mix[layer])
        if layer < N_A_LAYERS:
            i = layer
            h = h + conformer_conv(u, conv_w_in[i], conv_b_in[i], conv_dw[i], conv_ln_g[i],
                                   conv_ln_b[i], conv_w_out[i], conv_b_out[i])
        else:
            if layer == N_A_LAYERS:
                k_sh, v_sh = shared_kv(h, kv_norm, w_kv, k_norm)
            j = layer - N_A_LAYERS
            h = h + sliding_window_attention(u, w_q[j], q_norm[j], attn_sinks[j], w_o[j], k_sh, v_sh)
        h = h + swiglu(rms_norm(h, norm_ffn[layer]), ffn_w_gate[layer], ffn_w_up[layer], ffn_w_down[layer])
    return h[:, N_META:l]
```

```python
import functools

import jax
import jax.numpy as jnp
from jax import lax
from jax.experimental import pallas as pl
from jax.experimental.pallas import tpu as pltpu

N_META = 16
CONV_WIDTH = 31
HEAD_DIM = 64
N_KV_HEADS = 4
Q_PER_KV = 4
WINDOW = 128
BLOCK = 128
NORM_EPS = 1e-6
NEG_INF = -1e30

LANES = 128
HALO = 32
CONV_ROWS = 64
TOKEN_TILE = 512
VMEM_LIMIT = 56 * 1024 * 1024

_BF16 = jnp.bfloat16
_F32 = jnp.float32


def _const_spec(shape):
    return pl.BlockSpec(shape, lambda *_: (0,) * len(shape), pipeline_mode=pl.Buffered(1))


def _params(semantics):
    return pltpu.CompilerParams(dimension_semantics=semantics, vmem_limit_bytes=VMEM_LIMIT)


def _rms(x, g):
    ms = jnp.mean(x * x, axis=-1, keepdims=True)
    return x * lax.rsqrt(ms + NORM_EPS) * g


def _silu(x):
    return x * jax.nn.sigmoid(x)


def _dot(a, b):
    return jnp.dot(a, b, preferred_element_type=_F32)


def _dot_nt(a, b):
    return lax.dot_general(a, b, (((1,), (1,)), ((), ())), preferred_element_type=_F32)


def _conv_mixer_kernel(h_ref, g_ref, win_ref, bin_ref, dw_ref, lng_ref, lnb_ref, wout_ref, bout_ref,
                       o_ref, ext_ref, c_ref, *, seq_pad, seq_real):
    tm, d = h_ref.shape
    i = pl.program_id(0)

    @pl.when(i == 0)
    def _():
        ext_ref[0:HALO, :] = jnp.zeros((HALO, d), _F32)

    @pl.when(i > 0)
    def _():
        ext_ref[0:HALO, :] = ext_ref[tm:tm + HALO, :]

    h = h_ref[...]
    u = _rms(h, g_ref[...]).astype(_BF16)
    a2 = _dot(u, win_ref[...]) + bin_ref[...]
    a = a2[:, :d] * jax.nn.sigmoid(a2[:, d:])
    pos = (i * tm + lax.broadcasted_iota(jnp.int32, (tm, 1), 0)) % seq_pad
    ext_ref[HALO:, :] = jnp.where(pos < seq_real, a, 0.0)

    first_off = HALO - (CONV_WIDTH - 1)
    win_rows = CONV_ROWS + HALO

    def row_chunk(rc, carry):
        r0 = pl.multiple_of(rc * CONV_ROWS, CONV_ROWS)
        for cc in range(d // LANES):
            c0 = cc * LANES
            window = ext_ref[pl.ds(r0, win_rows), c0:c0 + LANES]
            acc = jnp.zeros((CONV_ROWS, LANES), _F32)
            for s in range(8):
                shifted = window if s == 0 else pltpu.roll(window, win_rows - s, axis=0)
                for m in range(HALO // 8 + 1):
                    j = 8 * m + s - first_off
                    if 0 <= j < CONV_WIDTH:
                        acc = acc + dw_ref[j:j + 1, c0:c0 + LANES] * shifted[8 * m:8 * m + CONV_ROWS, :]
            c_ref[pl.ds(r0, CONV_ROWS), c0:c0 + LANES] = acc
        return carry

    lax.fori_loop(0, tm // CONV_ROWS, row_chunk, 0)

    c = c_ref[...]
    mu = jnp.mean(c, axis=-1, keepdims=True)
    cen = c - mu
    var = jnp.mean(cen * cen, axis=-1, keepdims=True)
    y = cen * lax.rsqrt(var + NORM_EPS) * lng_ref[...] + lnb_ref[...]
    y = _silu(y).astype(_BF16)
    o_ref[...] = h + _dot(y, wout_ref[...]) + bout_ref[...]


def _conv_mixer(h, g, w_in, b_in, dw, ln_g, ln_b, w_out, b_out, *, seq_pad, seq_real):
    n, d = h.shape
    tm = TOKEN_TILE
    row = lambda i: (i, 0)
    return pl.pallas_call(
        functools.partial(_conv_mixer_kernel, seq_pad=seq_pad, seq_real=seq_real),
        grid=(n // tm,),
        in_specs=[
            pl.BlockSpec((tm, d), row),
            _const_spec((1, d)), _const_spec((d, 2 * d)), _const_spec((1, 2 * d)),
            _const_spec((CONV_WIDTH, d)), _const_spec((1, d)), _const_spec((1, d)),
            _const_spec((d, d)), _const_spec((1, d)),
        ],
        out_specs=pl.BlockSpec((tm, d), row),
        out_shape=jax.ShapeDtypeStruct((n, d), _F32),
        scratch_shapes=[pltpu.VMEM((tm + HALO, d), _F32), pltpu.VMEM((tm, d), _F32)],
        compiler_params=_params(("arbitrary",)),
        name="conv_mixer",
    )(h, g, w_in, b_in, dw, ln_g, ln_b, w_out, b_out)


def _ffn_kernel(h_ref, g_ref, wg_ref, wu_ref, wd_ref, o_ref):
    h = h_ref[...]
    u = _rms(h, g_ref[...]).astype(_BF16)
    z = (_silu(_dot(u, wg_ref[...])) * _dot(u, wu_ref[...])).astype(_BF16)
    o_ref[...] = h + _dot(z, wd_ref[...])


def _ffn(h, g, w_gate, w_up, w_down):
    n, d = h.shape
    f = w_gate.shape[1]
    tm = TOKEN_TILE
    row = lambda i: (i, 0)
    return pl.pallas_call(
        _ffn_kernel,
        grid=(n // tm,),
        in_specs=[pl.BlockSpec((tm, d), row), _const_spec((1, d)),
                  _const_spec((d, f)), _const_spec((d, f)), _const_spec((f, d))],
        out_specs=pl.BlockSpec((tm, d), row),
        out_shape=jax.ShapeDtypeStruct((n, d), _F32),
        compiler_params=_params(("parallel",)),
        name="ffn",
    )(h, g, w_gate, w_up, w_down)


def _kv_kernel(h_ref, g_ref, wkv_ref, kg_ref, k_ref, v_ref):
    kd = k_ref.shape[1]
    u = _rms(h_ref[...], g_ref[...]).astype(_BF16)
    kv = _dot(u, wkv_ref[...])
    for c in range(kd // LANES):
        k_ref[:, c * LANES:(c + 1) * LANES] = _rms(kv[:, c * LANES:(c + 1) * LANES], kg_ref[...]).astype(_BF16)
    v_ref[...] = kv[:, kd:].astype(_BF16)


def _shared_kv(h, g, w_kv_dup, k_gain_dup):
    n, d = h.shape
    kd = w_kv_dup.shape[1] // 2
    tm = TOKEN_TILE
    row = lambda i: (i, 0)
    return pl.pallas_call(
        _kv_kernel,
        grid=(n // tm,),
        in_specs=[pl.BlockSpec((tm, d), row), _const_spec((1, d)),
                  _const_spec((d, 2 * kd)), _const_spec((1, LANES))],
        out_specs=[pl.BlockSpec((tm, kd), row), pl.BlockSpec((tm, kd), row)],
        out_shape=[jax.ShapeDtypeStruct((n, kd), _BF16), jax.ShapeDtypeStruct((n, kd), _BF16)],
        compiler_params=_params(("parallel",)),
        name="shared_kv",
    )(h, g, w_kv_dup, k_gain_dup)


def _q_kernel(h_ref, g_ref, wq_ref, qg_ref, q_ref):
    tm, qd = q_ref.shape
    u = _rms(h_ref[...], g_ref[...]).astype(_BF16)
    q = _dot(u, wq_ref[...])
    low = lax.broadcasted_iota(jnp.int32, (tm, LANES), 1) < HEAD_DIM
    for c in range(qd // LANES):
        x = q[:, c * LANES:(c + 1) * LANES]
        sq = x * x
        ms_lo = jnp.sum(jnp.where(low, sq, 0.0), axis=-1, keepdims=True) * (1.0 / HEAD_DIM)
        ms_hi = jnp.sum(jnp.where(low, 0.0, sq), axis=-1, keepdims=True) * (1.0 / HEAD_DIM)
        inv = jnp.where(low, lax.rsqrt(ms_lo + NORM_EPS), lax.rsqrt(ms_hi + NORM_EPS))
        q_ref[:, c * LANES:(c + 1) * LANES] = (x * inv * qg_ref[...] * (HEAD_DIM ** -0.5)).astype(_BF16)


def _q_proj(h, g, w_q, q_gain_pair):
    n, d = h.shape
    qd = w_q.shape[1]
    tm = TOKEN_TILE
    row = lambda i: (i, 0)
    return pl.pallas_call(
        _q_kernel,
        grid=(n // tm,),
        in_specs=[pl.BlockSpec((tm, d), row), _const_spec((1, d)),
                  _const_spec((d, qd)), _const_spec((1, LANES))],
        out_specs=pl.BlockSpec((tm, qd), row),
        out_shape=jax.ShapeDtypeStruct((n, qd), _BF16),
        compiler_params=_params(("parallel",)),
        name="q_proj",
    )(h, g, w_q, q_gain_pair)


def _attn_kernel(sink_ref, q_ref, kp_ref, kc_ref, km_ref, vp_ref, vc_ref, vm_ref, o_ref):
    n = pl.program_id(1)
    rows = Q_PER_KV * BLOCK
    low = lax.broadcasted_iota(jnp.int32, (BLOCK, LANES), 1) < HEAD_DIM

    r = lax.broadcasted_iota(jnp.int32, (BLOCK, 2 * BLOCK), 0)
    c = lax.broadcasted_iota(jnp.int32, (BLOCK, 2 * BLOCK), 1)
    band_pos = (n - 1) * BLOCK + c
    band_ok = (c > r) & (c <= r + BLOCK) & (band_pos >= N_META)
    rm = lax.broadcasted_iota(jnp.int32, (BLOCK, N_META), 0)
    jm = lax.broadcasted_iota(jnp.int32, (BLOCK, N_META), 1)
    meta_ok = jm <= n * BLOCK + rm

    for g in range(N_KV_HEADS):
        gl = slice(g * LANES, (g + 1) * LANES)
        kb = jnp.concatenate([kp_ref[:, gl], kc_ref[:, gl]], axis=0)
        vb = jnp.concatenate([vp_ref[:, gl], vc_ref[:, gl]], axis=0)
        km = km_ref[:, gl]
        vm = vm_ref[:, gl]
        zero = jnp.zeros((BLOCK, LANES), _BF16)
        parts = []
        for p in range(Q_PER_KV // 2):
            qp = q_ref[:, (2 * g + p) * LANES:(2 * g + p + 1) * LANES]
            parts += [jnp.where(low, qp, zero), jnp.where(low, zero, qp)]
        qs = jnp.concatenate(parts, axis=0)
        s = _dot_nt(qs, kb).reshape(Q_PER_KV, BLOCK, 2 * BLOCK)
        sm = _dot_nt(qs, km).reshape(Q_PER_KV, BLOCK, N_META)
        s = jnp.where(band_ok[None], s, NEG_INF)
        sm = jnp.where(meta_ok[None], sm, NEG_INF)
        sink = jnp.stack([jnp.full((BLOCK, 1), sink_ref[g * Q_PER_KV + hh], _F32) for hh in range(Q_PER_KV)])
        mx = jnp.maximum(jnp.maximum(jnp.max(s, axis=-1, keepdims=True),
                                     jnp.max(sm, axis=-1, keepdims=True)), sink)
        pe = jnp.exp(s - mx)
        pme = jnp.exp(sm - mx)
        denom = jnp.sum(pe, axis=-1, keepdims=True) + jnp.sum(pme, axis=-1, keepdims=True) + jnp.exp(sink - mx)
        o = (_dot(pe.reshape(rows, 2 * BLOCK).astype(_BF16), vb)
             + _dot(pme.reshape(rows, N_META).astype(_BF16), vm))
        o = (o.reshape(Q_PER_KV, BLOCK, LANES) / denom).astype(_BF16)
        for p in range(Q_PER_KV // 2):
            o_ref[:, (2 * g + p) * LANES:(2 * g + p + 1) * LANES] = jnp.where(low, o[2 * p], o[2 * p + 1])


def _attention(sinks, q, k, v, *, batch, n_blocks):
    n, qd = q.shape
    kd = k.shape[1]
    meta_blocks = n_blocks * BLOCK // N_META
    cur = lambda b, j: (b * n_blocks + j, 0)
    prev = lambda b, j: (b * n_blocks + jnp.maximum(j - 1, 0), 0)
    meta = lambda b, j: (b * meta_blocks, 0)
    return pl.pallas_call(
        _attn_kernel,
        grid=(batch, n_blocks),
        in_specs=[
            pl.BlockSpec(memory_space=pltpu.SMEM),
            pl.BlockSpec((BLOCK, qd), cur),
            pl.BlockSpec((BLOCK, kd), prev), pl.BlockSpec((BLOCK, kd), cur), pl.BlockSpec((N_META, kd), meta),
            pl.BlockSpec((BLOCK, kd), prev), pl.BlockSpec((BLOCK, kd), cur), pl.BlockSpec((N_META, kd), meta),
        ],
        out_specs=pl.BlockSpec((BLOCK, qd), cur),
        out_shape=jax.ShapeDtypeStruct((n, qd), _BF16),
        compiler_params=_params(("parallel", "parallel")),
        name="swa_attention",
    )(sinks, q, k, k, k, v, v, v)


def _oproj_kernel(h_ref, a_ref, wo_ref, o_ref):
    o_ref[...] = h_ref[...] + _dot(a_ref[...], wo_ref[...])


def _o_proj(h, a, w_o):
    n, d = h.shape
    qd = a.shape[1]
    tm = TOKEN_TILE
    row = lambda i: (i, 0)
    return pl.pallas_call(
        _oproj_kernel,
        grid=(n // tm,),
        in_specs=[pl.BlockSpec((tm, d), row), pl.BlockSpec((tm, qd), row), _const_spec((qd, d))],
        out_specs=pl.BlockSpec((tm, d), row),
        out_shape=jax.ShapeDtypeStruct((n, d), _F32),
        compiler_params=_params(("parallel",)),
        name="o_proj",
    )(h, a, w_o)


def _dup_heads(w):
    lead = w.shape[:-1]
    w = w.reshape(lead + (N_KV_HEADS, 1, HEAD_DIM))
    return jnp.broadcast_to(w, lead + (N_KV_HEADS, 2, HEAD_DIM)).reshape(lead + (N_KV_HEADS * 2 * HEAD_DIM,))


def kernel(x, meta_tokens, norm_mix, norm_ffn, conv_w_in, conv_b_in, conv_dw, conv_ln_g, conv_ln_b,
           conv_w_out, conv_b_out, kv_norm, w_kv, k_norm, w_q, q_norm, attn_sinks, w_o,
           ffn_w_gate, ffn_w_up, ffn_w_down):
    b, seq, d = x.shape
    depth = norm_mix.shape[0]
    n_a = conv_w_in.shape[0]
    l = N_META + seq
    lp = -(-l // BLOCK) * BLOCK
    n_blocks = lp // BLOCK
    assert lp - l >= CONV_WIDTH - 1 and (b * lp) % TOKEN_TILE == 0

    meta = jnp.broadcast_to(meta_tokens.astype(x.dtype)[None], (b, N_META, d))
    h = jnp.concatenate([meta, x, jnp.zeros((b, lp - l, d), x.dtype)], axis=1).reshape(b * lp, d)

    row = lambda v: v.reshape(1, -1).astype(_F32)
    kvd = w_kv.shape[1] // 2
    w_kv_dup = jnp.concatenate([_dup_heads(w_kv[:, :kvd]), _dup_heads(w_kv[:, kvd:])], axis=1).astype(_BF16)
    k_gain_dup = jnp.tile(k_norm.astype(_F32), 2).reshape(1, LANES)

    k_sh = v_sh = None
    for layer in range(depth):
        if layer < n_a:
            i = layer
            h = _conv_mixer(h, row(norm_mix[layer]), conv_w_in[i].astype(_BF16), row(conv_b_in[i]),
                            conv_dw[i].astype(_F32), row(conv_ln_g[i]), row(conv_ln_b[i]),
                            conv_w_out[i].astype(_BF16), row(conv_b_out[i]), seq_pad=lp, seq_real=l)
        else:
            if layer == n_a:
                k_sh, v_sh = _shared_kv(h, row(kv_norm), w_kv_dup, k_gain_dup)
            j = layer - n_a
            q = _q_proj(h, row(norm_mix[layer]), w_q[j].astype(_BF16),
                        jnp.tile(q_norm[j].astype(_F32), 2).reshape(1, LANES))
            a = _attention(attn_sinks[j].astype(_F32), q, k_sh, v_sh, batch=b, n_blocks=n_blocks)
            h = _o_proj(h, a, w_o[j].astype(_BF16))
        h = _ffn(h, row(norm_ffn[layer]), ffn_w_gate[layer].astype(_BF16), ffn_w_up[layer].astype(_BF16),
                 ffn_w_down[layer].astype(_BF16))
    return h.reshape(b, lp, d)[:, N_META:l]
```

```python
import functools

import jax
import jax.numpy as jnp
from jax import lax
from jax.experimental import pallas as pl
from jax.experimental.pallas import tpu as pltpu

N_META = 16
CONV_WIDTH = 31
HEAD_DIM = 64
N_KV_HEADS = 4
Q_PER_KV = 4
WINDOW = 128
BLOCK = 128
NORM_EPS = 1e-6
NEG_INF = -1e30

LANES = 128
HALO = 32
CONV_ROWS = 64
TOKEN_TILE = 512
VMEM_LIMIT = 56 * 1024 * 1024

_BF16 = jnp.bfloat16
_F32 = jnp.float32


def _const_spec(shape):
    return pl.BlockSpec(shape, lambda *_: (0,) * len(shape), pipeline_mode=pl.Buffered(1))


def _params(semantics):
    return pltpu.CompilerParams(dimension_semantics=semantics, vmem_limit_bytes=VMEM_LIMIT)


def _rms(x, g):
    ms = jnp.mean(x * x, axis=-1, keepdims=True)
    return x * lax.rsqrt(ms + NORM_EPS) * g


def _silu(x):
    return x * jax.nn.sigmoid(x)


def _dot(a, b):
    return jnp.dot(a, b, preferred_element_type=_F32)


def _dot_nt(a, b):
    return lax.dot_general(a, b, (((1,), (1,)), ((), ())), preferred_element_type=_F32)


def _conv_mixer_kernel(h_ref, g_ref, win_ref, bin_ref, dw_ref, lng_ref, lnb_ref, wout_ref, bout_ref,
                       o_ref, ext_ref, c_ref, *, seq_pad, seq_real):
    tm, d = h_ref.shape
    i = pl.program_id(0)

    @pl.when(i == 0)
    def _():
        ext_ref[0:HALO, :] = jnp.zeros((HALO, d), _F32)

    @pl.when(i > 0)
    def _():
        ext_ref[0:HALO, :] = ext_ref[tm:tm + HALO, :]

    h = h_ref[...]
    u = _rms(h, g_ref[...]).astype(_BF16)
    a2 = _dot(u, win_ref[...]) + bin_ref[...]
    a = a2[:, :d] * jax.nn.sigmoid(a2[:, d:])
    pos = (i * tm + lax.broadcasted_iota(jnp.int32, (tm, 1), 0)) % seq_pad
    ext_ref[HALO:, :] = jnp.where(pos < seq_real, a, 0.0)

    first_off = HALO - (CONV_WIDTH - 1)
    win_rows = CONV_ROWS + HALO

    def row_chunk(rc, carry):
        r0 = pl.multiple_of(rc * CONV_ROWS, CONV_ROWS)
        for cc in range(d // LANES):
            c0 = cc * LANES
            window = ext_ref[pl.ds(r0, win_rows), c0:c0 + LANES]
            acc = jnp.zeros((CONV_ROWS, LANES), _F32)
            for s in range(8):
                shifted = window if s == 0 else pltpu.roll(window, win_rows - s, axis=0)
                for m in range(HALO // 8 + 1):
                    j = 8 * m + s - first_off
                    if 0 <= j < CONV_WIDTH:
                        acc = acc + dw_ref[j:j + 1, c0:c0 + LANES] * shifted[8 * m:8 * m + CONV_ROWS, :]
            c_ref[pl.ds(r0, CONV_ROWS), c0:c0 + LANES] = acc
        return carry

    lax.fori_loop(0, tm // CONV_ROWS, row_chunk, 0)

    c = c_ref[...]
    mu = jnp.mean(c, axis=-1, keepdims=True)
    cen = c - mu
    var = jnp.mean(cen * cen, axis=-1, keepdims=True)
    y = cen * lax.rsqrt(var + NORM_EPS) * lng_ref[...] + lnb_ref[...]
    y = _silu(y).astype(_BF16)
    o_ref[...] = h + _dot(y, wout_ref[...]) + bout_ref[...]


def _conv_mixer(h, g, w_in, b_in, dw, ln_g, ln_b, w_out, b_out, *, seq_pad, seq_real):
    n, d = h.shape
    tm = TOKEN_TILE
    row = lambda i: (i, 0)
    return pl.pallas_call(
        functools.partial(_conv_mixer_kernel, seq_pad=seq_pad, seq_real=seq_real),
        grid=(n // tm,),
        in_specs=[
            pl.BlockSpec((tm, d), row),
            _const_spec((1, d)), _const_spec((d, 2 * d)), _const_spec((1, 2 * d)),
            _const_spec((CONV_WIDTH, d)), _const_spec((1, d)), _const_spec((1, d)),
            _const_spec((d, d)), _const_spec((1, d)),
        ],
        out_specs=pl.BlockSpec((tm, d), row),
        out_shape=jax.ShapeDtypeStruct((n, d), _F32),
        scratch_shapes=[pltpu.VMEM((tm + HALO, d), _F32), pltpu.VMEM((tm, d), _F32)],
        compiler_params=_params(("arbitrary",)),
        name="conv_mixer",
    )(h, g, w_in, b_in, dw, ln_g, ln_b, w_out, b_out)


def _ffn_body(h, g_ref, wg_ref, wu_ref, wd_ref):
    u = _rms(h, g_ref[...]).astype(_BF16)
    z = (_silu(_dot(u, wg_ref[...])) * _dot(u, wu_ref[...])).astype(_BF16)
    return h + _dot(z, wd_ref[...])


def _ffn_kernel(h_ref, g_ref, wg_ref, wu_ref, wd_ref, o_ref):
    o_ref[...] = _ffn_body(h_ref[...], g_ref, wg_ref, wu_ref, wd_ref)


def _ffn(h, g, w_gate, w_up, w_down):
    n, d = h.shape
    f = w_gate.shape[1]
    tm = TOKEN_TILE
    row = lambda i: (i, 0)
    return pl.pallas_call(
        _ffn_kernel,
        grid=(n // tm,),
        in_specs=[pl.BlockSpec((tm, d), row), _const_spec((1, d)),
                  _const_spec((d, f)), _const_spec((d, f)), _const_spec((f, d))],
        out_specs=pl.BlockSpec((tm, d), row),
        out_shape=jax.ShapeDtypeStruct((n, d), _F32),
        compiler_params=_params(("parallel",)),
        name="ffn",
    )(h, g, w_gate, w_up, w_down)


def _oproj_ffn_kernel(h_ref, a_ref, wo_ref, g_ref, wg_ref, wu_ref, wd_ref, o_ref):
    h = h_ref[...] + _dot(a_ref[...], wo_ref[...])
    o_ref[...] = _ffn_body(h, g_ref, wg_ref, wu_ref, wd_ref)


def _oproj_ffn(h, a, w_o, g, w_gate, w_up, w_down):
    n, d = h.shape
    qd = a.shape[1]
    f = w_gate.shape[1]
    tm = TOKEN_TILE
    row = lambda i: (i, 0)
    return pl.pallas_call(
        _oproj_ffn_kernel,
        grid=(n // tm,),
        in_specs=[pl.BlockSpec((tm, d), row), pl.BlockSpec((tm, qd), row), _const_spec((qd, d)),
                  _const_spec((1, d)), _const_spec((d, f)), _const_spec((d, f)), _const_spec((f, d))],
        out_specs=pl.BlockSpec((tm, d), row),
        out_shape=jax.ShapeDtypeStruct((n, d), _F32),
        compiler_params=_params(("parallel",)),
        name="oproj_ffn",
    )(h, a, w_o, g, w_gate, w_up, w_down)


def _kv_kernel(h_ref, g_ref, wk_ref, wvt_ref, kg_ref, k_ref, vt_ref):
    kd = k_ref.shape[1]
    u = _rms(h_ref[...], g_ref[...]).astype(_BF16)
    k = _dot(u, wk_ref[...])
    for c in range(kd // LANES):
        k_ref[:, c * LANES:(c + 1) * LANES] = _rms(k[:, c * LANES:(c + 1) * LANES], kg_ref[...]).astype(_BF16)
    vt_ref[...] = _dot_nt(wvt_ref[...], u).astype(_BF16)


def _shared_kv(h, g, w_k_dup, w_vt_dup, k_gain_dup):
    n, d = h.shape
    kd = w_k_dup.shape[1]
    tm = TOKEN_TILE
    row = lambda i: (i, 0)
    return pl.pallas_call(
        _kv_kernel,
        grid=(n // tm,),
        in_specs=[pl.BlockSpec((tm, d), row), _const_spec((1, d)),
                  _const_spec((d, kd)), _const_spec((kd, d)), _const_spec((1, LANES))],
        out_specs=[pl.BlockSpec((tm, kd), row), pl.BlockSpec((kd, tm), lambda i: (0, i))],
        out_shape=[jax.ShapeDtypeStruct((n, kd), _BF16), jax.ShapeDtypeStruct((kd, n), _BF16)],
        compiler_params=_params(("parallel",)),
        name="shared_kv",
    )(h, g, w_k_dup, w_vt_dup, k_gain_dup)


def _q_kernel(h_ref, g_ref, wq_ref, qg_ref, q_ref):
    tm, qd = q_ref.shape
    u = _rms(h_ref[...], g_ref[...]).astype(_BF16)
    q = _dot(u, wq_ref[...])
    low = lax.broadcasted_iota(jnp.int32, (tm, LANES), 1) < HEAD_DIM
    for c in range(qd // LANES):
        x = q[:, c * LANES:(c + 1) * LANES]
        sq = x * x
        ms_lo = jnp.sum(jnp.where(low, sq, 0.0), axis=-1, keepdims=True) * (1.0 / HEAD_DIM)
        ms_hi = jnp.sum(jnp.where(low, 0.0, sq), axis=-1, keepdims=True) * (1.0 / HEAD_DIM)
        inv = jnp.where(low, lax.rsqrt(ms_lo + NORM_EPS), lax.rsqrt(ms_hi + NORM_EPS))
        q_ref[:, c * LANES:(c + 1) * LANES] = (x * inv * qg_ref[...] * (HEAD_DIM ** -0.5)).astype(_BF16)


def _q_proj(h, g, w_q, q_gain_pair):
    n, d = h.shape
    qd = w_q.shape[1]
    tm = TOKEN_TILE
    row = lambda i: (i, 0)
    return pl.pallas_call(
        _q_kernel,
        grid=(n // tm,),
        in_specs=[pl.BlockSpec((tm, d), row), _const_spec((1, d)),
                  _const_spec((d, qd)), _const_spec((1, LANES))],
        out_specs=pl.BlockSpec((tm, qd), row),
        out_shape=jax.ShapeDtypeStruct((n, qd), _BF16),
        compiler_params=_params(("parallel",)),
        name="q_proj",
    )(h, g, w_q, q_gain_pair)


def _attn_kernel(sink_ref, q_ref, kp_ref, kc_ref, km_ref, vtp_ref, vtc_ref, vtm_ref, o_ref):
    n = pl.program_id(1)
    nq = Q_PER_KV * BLOCK
    low = lax.broadcasted_iota(jnp.int32, (BLOCK, LANES), 1) < HEAD_DIM
    row_low = lax.broadcasted_iota(jnp.int32, (LANES, BLOCK), 0) < HEAD_DIM

    c = lax.broadcasted_iota(jnp.int32, (BLOCK, nq), 0)
    r = lax.broadcasted_iota(jnp.int32, (BLOCK, nq), 1) & (BLOCK - 1)
    use_cur = c <= r
    band_ok = jnp.where(use_cur, n * BLOCK + c, (n - 1) * BLOCK + c) >= N_META
    jm = lax.broadcasted_iota(jnp.int32, (N_META, nq), 0)
    rm = lax.broadcasted_iota(jnp.int32, (N_META, nq), 1) & (BLOCK - 1)
    meta_ok = jm <= n * BLOCK + rm
    zero_bf = jnp.zeros((BLOCK, LANES), _BF16)
    meta_pad = jnp.zeros((BLOCK - N_META, nq), _BF16)

    for g in range(N_KV_HEADS):
        gl = slice(g * LANES, (g + 1) * LANES)
        parts = []
        for p in range(Q_PER_KV // 2):
            qp = q_ref[:, (2 * g + p) * LANES:(2 * g + p + 1) * LANES]
            parts += [jnp.where(low, qp, zero_bf), jnp.where(low, zero_bf, qp)]
        qs = jnp.concatenate(parts, axis=0)
        kb = jnp.concatenate([kp_ref[:, gl], kc_ref[:, gl]], axis=0)
        st = _dot_nt(kb, qs)
        s = jnp.where(band_ok, jnp.where(use_cur, st[BLOCK:], st[:BLOCK]), NEG_INF)
        sm = jnp.where(meta_ok, _dot_nt(km_ref[:, gl], qs), NEG_INF)
        sink = jnp.concatenate(
            [jnp.full((1, BLOCK), sink_ref[g * Q_PER_KV + hh], _F32) for hh in range(Q_PER_KV)], axis=1)
        mx = jnp.maximum(jnp.maximum(jnp.max(s, axis=0, keepdims=True),
                                     jnp.max(sm, axis=0, keepdims=True)), sink)
        pe = jnp.exp(s - mx)
        pme = jnp.exp(sm - mx)
        denom = jnp.sum(pe, axis=0, keepdims=True) + jnp.sum(pme, axis=0, keepdims=True) + jnp.exp(sink - mx)
        pt = jnp.concatenate([jnp.where(use_cur, 0.0, pe).astype(_BF16), jnp.where(use_cur, pe, 0.0).astype(_BF16),
                              pme.astype(_BF16), meta_pad], axis=0)
        vt = jnp.concatenate([vtp_ref[gl, :], vtc_ref[gl, :], vtm_ref[gl, :]], axis=1)
        ot = _dot(vt, pt) / denom
        for p in range(Q_PER_KV // 2):
            pair_t = jnp.where(row_low, ot[:, 2 * p * BLOCK:(2 * p + 1) * BLOCK],
                               ot[:, (2 * p + 1) * BLOCK:(2 * p + 2) * BLOCK])
            o_ref[:, (2 * g + p) * LANES:(2 * g + p + 1) * LANES] = pair_t.T.astype(_BF16)


def _attention(sinks, q, k, vt, *, batch, n_blocks):
    n, qd = q.shape
    kd = k.shape[1]
    meta_blocks = n_blocks * BLOCK // N_META
    cur = lambda b, j: (b * n_blocks + j, 0)
    prev = lambda b, j: (b * n_blocks + jnp.maximum(j - 1, 0), 0)
    meta = lambda b, j: (b * meta_blocks, 0)
    cur_t = lambda b, j: (0, b * n_blocks + j)
    prev_t = lambda b, j: (0, b * n_blocks + jnp.maximum(j - 1, 0))
    first_t = lambda b, j: (0, b * n_blocks)
    return pl.pallas_call(
        _attn_kernel,
        grid=(batch, n_blocks),
        in_specs=[
            pl.BlockSpec(memory_space=pltpu.SMEM),
            pl.BlockSpec((BLOCK, qd), cur),
            pl.BlockSpec((BLOCK, kd), prev), pl.BlockSpec((BLOCK, kd), cur), pl.BlockSpec((N_META, kd), meta),
            pl.BlockSpec((kd, BLOCK), prev_t), pl.BlockSpec((kd, BLOCK), cur_t), pl.BlockSpec((kd, BLOCK), first_t),
        ],
        out_specs=pl.BlockSpec((BLOCK, qd), cur),
        out_shape=jax.ShapeDtypeStruct((n, qd), _BF16),
        compiler_params=_params(("parallel", "parallel")),
        name="swa_attention",
    )(sinks, q, k, k, k, vt, vt, vt)


def _dup_heads(w):
    lead = w.shape[:-1]
    w = w.reshape(lead + (N_KV_HEADS, 1, HEAD_DIM))
    return jnp.broadcast_to(w, lead + (N_KV_HEADS, 2, HEAD_DIM)).reshape(lead + (N_KV_HEADS * 2 * HEAD_DIM,))


def kernel(x, meta_tokens, norm_mix, norm_ffn, conv_w_in, conv_b_in, conv_dw, conv_ln_g, conv_ln_b,
           conv_w_out, conv_b_out, kv_norm, w_kv, k_norm, w_q, q_norm, attn_sinks, w_o,
           ffn_w_gate, ffn_w_up, ffn_w_down):
    b, seq, d = x.shape
    depth = norm_mix.shape[0]
    n_a = conv_w_in.shape[0]
    l = N_META + seq
    lp = -(-l // BLOCK) * BLOCK
    n_blocks = lp // BLOCK
    assert lp - l >= CONV_WIDTH - 1 and (b * lp) % TOKEN_TILE == 0

    meta = jnp.broadcast_to(meta_tokens.astype(x.dtype)[None], (b, N_META, d))
    h = jnp.concatenate([meta, x, jnp.zeros((b, lp - l, d), x.dtype)], axis=1).reshape(b * lp, d)

    row = lambda v: v.reshape(1, -1).astype(_F32)
    kvd = w_kv.shape[1] // 2
    w_k_dup = _dup_heads(w_kv[:, :kvd]).astype(_BF16)
    w_vt_dup = _dup_heads(w_kv[:, kvd:]).T.astype(_BF16)
    k_gain_dup = jnp.tile(k_norm.astype(_F32), 2).reshape(1, LANES)

    k_sh = vt_sh = None
    for layer in range(depth):
        ffn_w = (row(norm_ffn[layer]), ffn_w_gate[layer].astype(_BF16), ffn_w_up[layer].astype(_BF16),
                 ffn_w_down[layer].astype(_BF16))
        if layer < n_a:
            i = layer
            h = _conv_mixer(h, row(norm_mix[layer]), conv_w_in[i].astype(_BF16), row(conv_b_in[i]),
                            conv_dw[i].astype(_F32), row(conv_ln_g[i]), row(conv_ln_b[i]),
                            conv_w_out[i].astype(_BF16), row(conv_b_out[i]), seq_pad=lp, seq_real=l)
            h = _ffn(h, *ffn_w)
        else:
            if layer == n_a:
                k_sh, vt_sh = _shared_kv(h, row(kv_norm), w_k_dup, w_vt_dup, k_gain_dup)
            j = layer - n_a
            q = _q_proj(h, row(norm_mix[layer]), w_q[j].astype(_BF16),
                        jnp.tile(q_norm[j].astype(_F32), 2).reshape(1, LANES))
            a = _attention(attn_sinks[j].astype(_F32), q, k_sh, vt_sh, batch=b, n_blocks=n_blocks)
            h = _oproj_ffn(h, a, w_o[j].astype(_BF16), *ffn_w)
    return h.reshape(b, lp, d)[:, N_META:l]
```

```python
import functools

import jax
import jax.numpy as jnp
from jax import lax
from jax.experimental import pallas as pl
from jax.experimental.pallas import tpu as pltpu

N_META = 16
CONV_WIDTH = 31
HEAD_DIM = 64
N_KV_HEADS = 4
Q_PER_KV = 4
BLOCK = 128
NORM_EPS = 1e-6
NEG_INF = -1e30

LANES = 128
HALO = 32
CONV_ROWS = 64
TOKEN_TILE = 512
VMEM_LIMIT = 56 * 1024 * 1024

_BF16 = jnp.bfloat16
_F32 = jnp.float32


def _const_spec(shape):
    return pl.BlockSpec(shape, lambda *_: (0,) * len(shape), pipeline_mode=pl.Buffered(1))


def _params(semantics):
    return pltpu.CompilerParams(dimension_semantics=semantics, vmem_limit_bytes=VMEM_LIMIT)


def _rms(x, g):
    ms = jnp.mean(x * x, axis=-1, keepdims=True)
    return x * lax.rsqrt(ms + NORM_EPS) * g


def _silu(x):
    return x * jax.nn.sigmoid(x)


def _dot(a, b):
    return jnp.dot(a, b, preferred_element_type=_F32)


def _dot_nt(a, b):
    return lax.dot_general(a, b, (((1,), (1,)), ((), ())), preferred_element_type=_F32)


def _conv_mixer_kernel(h_ref, halo_ref, g_ref, win_ref, bin_ref, dw_ref, lng_ref, lnb_ref, wout_ref, bout_ref,
                       o_ref, tail_ref, ext_ref, c_ref, *, tiles_per_seq):
    tm, d = h_ref.shape
    first = pl.program_id(0) % tiles_per_seq == 0

    @pl.when(first)
    def _():
        ext_ref[0:HALO, :] = halo_ref[...]

    @pl.when(jnp.logical_not(first))
    def _():
        ext_ref[0:HALO, :] = ext_ref[tm:tm + HALO, :]

    h = h_ref[...]
    u = _rms(h, g_ref[...]).astype(_BF16)
    a2 = _dot(u, win_ref[...]) + bin_ref[...]
    ext_ref[HALO:, :] = a2[:, :d] * jax.nn.sigmoid(a2[:, d:])
    tail_ref[...] = ext_ref[tm:tm + HALO, :]

    first_off = HALO - (CONV_WIDTH - 1)
    conv_rows = min(CONV_ROWS, tm)
    win_rows = conv_rows + HALO

    def row_chunk(rc, carry):
        r0 = pl.multiple_of(rc * conv_rows, conv_rows)
        for cc in range(d // LANES):
            c0 = cc * LANES
            window = ext_ref[pl.ds(r0, win_rows), c0:c0 + LANES]
            acc = jnp.zeros((conv_rows, LANES), _F32)
            for s in range(8):
                shifted = window if s == 0 else pltpu.roll(window, win_rows - s, axis=0)
                for m in range(HALO // 8 + 1):
                    j = 8 * m + s - first_off
                    if 0 <= j < CONV_WIDTH:
                        acc = acc + dw_ref[j:j + 1, c0:c0 + LANES] * shifted[8 * m:8 * m + conv_rows, :]
            c_ref[pl.ds(r0, conv_rows), c0:c0 + LANES] = acc
        return carry

    lax.fori_loop(0, tm // conv_rows, row_chunk, 0)

    c = c_ref[...]
    mu = jnp.mean(c, axis=-1, keepdims=True)
    cen = c - mu
    var = jnp.mean(cen * cen, axis=-1, keepdims=True)
    y = cen * lax.rsqrt(var + NORM_EPS) * lng_ref[...] + lnb_ref[...]
    y = _silu(y).astype(_BF16)
    o_ref[...] = h + _dot(y, wout_ref[...]) + bout_ref[...]


def _conv_mixer(h, start_halo, w, *, tiles_per_seq):
    n, d = h.shape
    tm = n // (n // min(TOKEN_TILE, n))
    row = lambda i: (i, 0)
    return pl.pallas_call(
        functools.partial(_conv_mixer_kernel, tiles_per_seq=tiles_per_seq),
        grid=(n // tm,),
        in_specs=[
            pl.BlockSpec((tm, d), row), _const_spec((HALO, d)),
            _const_spec((1, d)), _const_spec((d, 2 * d)), _const_spec((1, 2 * d)),
            _const_spec((CONV_WIDTH, d)), _const_spec((1, d)), _const_spec((1, d)),
            _const_spec((d, d)), _const_spec((1, d)),
        ],
        out_specs=[pl.BlockSpec((tm, d), row), pl.BlockSpec((HALO, d), lambda i: (0, 0))],
        out_shape=[jax.ShapeDtypeStruct((n, d), _F32), jax.ShapeDtypeStruct((HALO, d), _F32)],
        scratch_shapes=[pltpu.VMEM((tm + HALO, d), _F32), pltpu.VMEM((tm, d), _F32)],
        compiler_params=_params(("arbitrary",)),
        name="conv_mixer",
    )(h, start_halo, *w)


def _head_pair_rms(x, gain_pair, scale):
    low = lax.broadcasted_iota(jnp.int32, x.shape, 1) < HEAD_DIM
    sq = x * x
    ms_lo = jnp.sum(jnp.where(low, sq, 0.0), axis=-1, keepdims=True) * (1.0 / HEAD_DIM)
    ms_hi = jnp.sum(jnp.where(low, 0.0, sq), axis=-1, keepdims=True) * (1.0 / HEAD_DIM)
    inv = jnp.where(low, lax.rsqrt(ms_lo + NORM_EPS), lax.rsqrt(ms_hi + NORM_EPS))
    return x * inv * gain_pair * scale


def _ffn_kernel(*refs, pre_oproj, post_q, post_kv):
    refs = list(refs)
    h_ref = refs.pop(0)
    h = h_ref[...]
    if pre_oproj:
        a_ref, wo_ref = refs.pop(0), refs.pop(0)
        h = h + _dot(a_ref[...], wo_ref[...])
    g_ref, wg_ref, wu_ref, wd_ref = (refs.pop(0) for _ in range(4))
    if post_q:
        gq_ref, wq_ref, qg_ref = (refs.pop(0) for _ in range(3))
    if post_kv:
        gkv_ref, wk_ref, wvt_ref, kg_ref = (refs.pop(0) for _ in range(4))
    o_ref = refs.pop(0)

    u = _rms(h, g_ref[...]).astype(_BF16)
    z = (_silu(_dot(u, wg_ref[...])) * _dot(u, wu_ref[...])).astype(_BF16)
    h = h + _dot(z, wd_ref[...])
    o_ref[...] = h

    if post_q:
        q_ref = refs.pop(0)
        q = _dot(_rms(h, gq_ref[...]).astype(_BF16), wq_ref[...])
        for c in range(q_ref.shape[1] // LANES):
            cl = slice(c * LANES, (c + 1) * LANES)
            q_ref[:, cl] = _head_pair_rms(q[:, cl], qg_ref[...], HEAD_DIM ** -0.5).astype(_BF16)
    if post_kv:
        k_ref, vt_ref = refs.pop(0), refs.pop(0)
        ukv = _rms(h, gkv_ref[...]).astype(_BF16)
        k = _dot(ukv, wk_ref[...])
        for c in range(k_ref.shape[1] // LANES):
            cl = slice(c * LANES, (c + 1) * LANES)
            k_ref[:, cl] = _rms(k[:, cl], kg_ref[...]).astype(_BF16)
        vt_ref[...] = _dot_nt(wvt_ref[...], ukv).astype(_BF16)


def _ffn(h, ffn_w, *, oproj=None, q_w=None, kv_w=None):
    n, d = h.shape
    f = ffn_w[1].shape[1]
    tm = n // (n // min(TOKEN_TILE, n))
    row = lambda i: (i, 0)
    args, in_specs = [h], [pl.BlockSpec((tm, d), row)]
    if oproj is not None:
        args += list(oproj)
        in_specs += [pl.BlockSpec((tm, oproj[0].shape[1]), row), _const_spec(oproj[1].shape)]
    args += list(ffn_w)
    in_specs += [_const_spec((1, d)), _const_spec((d, f)), _const_spec((d, f)), _const_spec((f, d))]
    out_specs = [pl.BlockSpec((tm, d), row)]
    out_shape = [jax.ShapeDtypeStruct((n, d), _F32)]
    if q_w is not None:
        qd = q_w[1].shape[1]
        args += list(q_w)
        in_specs += [_const_spec(a.shape) for a in q_w]
        out_specs.append(pl.BlockSpec((tm, qd), row))
        out_shape.append(jax.ShapeDtypeStruct((n, qd), _BF16))
    if kv_w is not None:
        kd = kv_w[1].shape[1]
        args += list(kv_w)
        in_specs += [_const_spec(a.shape) for a in kv_w]
        out_specs += [pl.BlockSpec((tm, kd), row), pl.BlockSpec((kd, tm), lambda i: (0, i))]
        out_shape += [jax.ShapeDtypeStruct((n, kd), _BF16), jax.ShapeDtypeStruct((kd, n), _BF16)]
    return pl.pallas_call(
        functools.partial(_ffn_kernel, pre_oproj=oproj is not None, post_q=q_w is not None,
                          post_kv=kv_w is not None),
        grid=(n // tm,),
        in_specs=in_specs, out_specs=out_specs, out_shape=out_shape,
        compiler_params=_params(("parallel",)),
        name="ffn",
    )(*args)


def _attn_kernel(sink_ref, q_ref, kp_ref, kc_ref, km_ref, vtp_ref, vtc_ref, vtm_ref, o_ref):
    n = pl.program_id(1)
    nq = Q_PER_KV * BLOCK
    low = lax.broadcasted_iota(jnp.int32, (BLOCK, LANES), 1) < HEAD_DIM
    row_low = lax.broadcasted_iota(jnp.int32, (LANES, BLOCK), 0) < HEAD_DIM

    c = lax.broadcasted_iota(jnp.int32, (BLOCK, nq), 0)
    r = lax.broadcasted_iota(jnp.int32, (BLOCK, nq), 1) & (BLOCK - 1)
    use_cur = c <= r
    band_ok = use_cur | (n > 0)
    zero_bf = jnp.zeros((BLOCK, LANES), _BF16)
    meta_pad = jnp.zeros((BLOCK - N_META, nq), _BF16)

    for g in range(N_KV_HEADS):
        gl = slice(g * LANES, (g + 1) * LANES)
        parts = []
        for p in range(Q_PER_KV // 2):
            qp = q_ref[:, (2 * g + p) * LANES:(2 * g + p + 1) * LANES]
            parts += [jnp.where(low, qp, zero_bf), jnp.where(low, zero_bf, qp)]
        qs = jnp.concatenate(parts, axis=0)
        kb = jnp.concatenate([kp_ref[:, gl], kc_ref[:, gl]], axis=0)
        st = _dot_nt(kb, qs)
        s = jnp.where(band_ok, jnp.where(use_cur, st[BLOCK:], st[:BLOCK]), NEG_INF)
        sm = _dot_nt(km_ref[:, gl], qs)
        sink = jnp.concatenate(
            [jnp.full((1, BLOCK), sink_ref[g * Q_PER_KV + hh], _F32) for hh in range(Q_PER_KV)], axis=1)
        mx = jnp.maximum(jnp.maximum(jnp.max(s, axis=0, keepdims=True),
                                     jnp.max(sm, axis=0, keepdims=True)), sink)
        pe = jnp.exp(s - mx)
        pme = jnp.exp(sm - mx)
        denom = jnp.sum(pe, axis=0, keepdims=True) + jnp.sum(pme, axis=0, keepdims=True) + jnp.exp(sink - mx)
        pt = jnp.concatenate([jnp.where(use_cur, 0.0, pe).astype(_BF16), jnp.where(use_cur, pe, 0.0).astype(_BF16),
                              pme.astype(_BF16), meta_pad], axis=0)
        vt = jnp.concatenate([vtp_ref[gl, :], vtc_ref[gl, :], vtm_ref[gl, :]], axis=1)
        ot = _dot(vt, pt) / denom
        for p in range(Q_PER_KV // 2):
            pair_t = jnp.where(row_low, ot[:, 2 * p * BLOCK:(2 * p + 1) * BLOCK],
                               ot[:, (2 * p + 1) * BLOCK:(2 * p + 2) * BLOCK])
            o_ref[:, (2 * g + p) * LANES:(2 * g + p + 1) * LANES] = pair_t.T.astype(_BF16)


def _attention(sinks, q, k, vt, k_meta, vt_meta, *, batch, n_blocks):
    n, qd = q.shape
    kd = k.shape[1]
    cur = lambda b, j: (b * n_blocks + j, 0)
    prev = lambda b, j: (b * n_blocks + jnp.maximum(j - 1, 0), 0)
    cur_t = lambda b, j: (0, b * n_blocks + j)
    prev_t = lambda b, j: (0, b * n_blocks + jnp.maximum(j - 1, 0))
    return pl.pallas_call(
        _attn_kernel,
        grid=(batch, n_blocks),
        in_specs=[
            pl.BlockSpec(memory_space=pltpu.SMEM),
            pl.BlockSpec((BLOCK, qd), cur),
            pl.BlockSpec((BLOCK, kd), prev), pl.BlockSpec((BLOCK, kd), cur), _const_spec((N_META, kd)),
            pl.BlockSpec((kd, BLOCK), prev_t), pl.BlockSpec((kd, BLOCK), cur_t), _const_spec((kd, BLOCK)),
        ],
        out_specs=pl.BlockSpec((BLOCK, qd), cur),
        out_shape=jax.ShapeDtypeStruct((n, qd), _BF16),
        compiler_params=_params(("parallel", "parallel")),
        name="swa_attention",
    )(sinks, q, k, k, k_meta, vt, vt, vt_meta)


def _dup_heads(w):
    lead = w.shape[:-1]
    w = w.reshape(lead + (N_KV_HEADS, 1, HEAD_DIM))
    return jnp.broadcast_to(w, lead + (N_KV_HEADS, 2, HEAD_DIM)).reshape(lead + (N_KV_HEADS * 2 * HEAD_DIM,))


def kernel(x, meta_tokens, norm_mix, norm_ffn, conv_w_in, conv_b_in, conv_dw, conv_ln_g, conv_ln_b,
           conv_w_out, conv_b_out, kv_norm, w_kv, k_norm, w_q, q_norm, attn_sinks, w_o,
           ffn_w_gate, ffn_w_up, ffn_w_down):
    b, seq, d = x.shape
    depth = norm_mix.shape[0]
    n_a = conv_w_in.shape[0]
    assert seq % TOKEN_TILE == 0 and seq % BLOCK == 0 and n_a >= 1 and depth > n_a
    assert CONV_WIDTH - 1 <= HALO and N_META <= HALO and N_META % 8 == 0

    row = lambda v: v.reshape(1, -1).astype(_F32)
    kvd = w_kv.shape[1] // 2
    kv_w = (row(kv_norm), _dup_heads(w_kv[:, :kvd]).astype(_BF16), _dup_heads(w_kv[:, kvd:]).T.astype(_BF16),
            jnp.tile(k_norm.astype(_F32), 2).reshape(1, LANES))
    q_ws = [(row(norm_mix[n_a + j]), w_q[j].astype(_BF16), jnp.tile(q_norm[j].astype(_F32), 2).reshape(1, LANES))
            for j in range(depth - n_a)]
    ffn_ws = [(row(norm_ffn[i]), ffn_w_gate[i].astype(_BF16), ffn_w_up[i].astype(_BF16),
               ffn_w_down[i].astype(_BF16)) for i in range(depth)]
    conv_ws = [(row(norm_mix[i]), conv_w_in[i].astype(_BF16), row(conv_b_in[i]), conv_dw[i].astype(_F32),
                row(conv_ln_g[i]), row(conv_ln_b[i]), conv_w_out[i].astype(_BF16), row(conv_b_out[i]))
               for i in range(n_a)]

    hm = meta_tokens.astype(_F32)
    h = x.reshape(b * seq, d)
    no_history = jnp.zeros((HALO, d), _F32)
    k_meta = vt_meta = k_sh = vt_sh = q = None
    for i in range(n_a):
        last = i == n_a - 1
        hm, meta_tail = _conv_mixer(hm, no_history, conv_ws[i], tiles_per_seq=1)
        h, _ = _conv_mixer(h, meta_tail, conv_ws[i], tiles_per_seq=seq // TOKEN_TILE)
        if last:
            _, _, k_meta, vt_meta = _ffn(hm, ffn_ws[i], q_w=q_ws[0], kv_w=kv_w)
            h, q, k_sh, vt_sh = _ffn(h, ffn_ws[i], q_w=q_ws[0], kv_w=kv_w)
        else:
            hm, = _ffn(hm, ffn_ws[i])
            h, = _ffn(h, ffn_ws[i])
    vt_meta = jnp.pad(vt_meta, ((0, 0), (0, BLOCK - N_META)))
    for j in range(depth - n_a):
        layer = n_a + j
        a = _attention(attn_sinks[j].astype(_F32), q, k_sh, vt_sh, k_meta, vt_meta, batch=b, n_blocks=seq // BLOCK)
        oproj = (a, w_o[j].astype(_BF16))
        if layer + 1 < depth:
            h, q = _ffn(h, ffn_ws[layer], oproj=oproj, q_w=q_ws[j + 1])
        else:
            h, = _ffn(h, ffn_ws[layer], oproj=oproj)
    return h.reshape(b, seq, d)
```

```python
import functools

import jax
import jax.numpy as jnp
from jax import lax
from jax.experimental import pallas as pl
from jax.experimental.pallas import tpu as pltpu

N_META = 16
CONV_WIDTH = 31
HEAD_DIM = 64
N_KV_HEADS = 4
Q_PER_KV = 4
BLOCK = 128
NORM_EPS = 1e-6
NEG_INF = -1e30

LANES = 128
CONV_CHANNELS = 16
EXTRA_ROWS = 16
TOKEN_TILE = 512
VMEM_LIMIT = 56 * 1024 * 1024

_BF16 = jnp.bfloat16
_F32 = jnp.float32


def _const_spec(shape):
    return pl.BlockSpec(shape, lambda *_: (0,) * len(shape), pipeline_mode=pl.Buffered(1))


def _params(semantics):
    return pltpu.CompilerParams(dimension_semantics=semantics, vmem_limit_bytes=VMEM_LIMIT)


def _rms(x, g):
    ms = jnp.mean(x * x, axis=-1, keepdims=True)
    return x * lax.rsqrt(ms + NORM_EPS) * g


def _silu(x):
    return x * jax.nn.sigmoid(x)


def _dot(a, b):
    return jnp.dot(a, b, preferred_element_type=_F32)


def _dot_nt(a, b):
    return lax.dot_general(a, b, (((1,), (1,)), ((), ())), preferred_element_type=_F32)


def _glu_t_kernel(h_ref, g_ref, wint_ref, bint_ref, at_ref):
    d = h_ref.shape[1]
    u = _rms(h_ref[...], g_ref[...]).astype(_BF16)
    a2t = _dot_nt(wint_ref[...], u) + bint_ref[...]
    at_ref[...] = (a2t[:d] * jax.nn.sigmoid(a2t[d:])).astype(_BF16)


def _glu_t(h, g, w_in_t, b_in_t):
    n, d = h.shape
    tm = min(TOKEN_TILE, n)
    return pl.pallas_call(
        _glu_t_kernel,
        grid=(n // tm,),
        in_specs=[pl.BlockSpec((tm, d), lambda i: (i, 0)), _const_spec((1, d)),
                  _const_spec((2 * d, d)), _const_spec((2 * d, 1))],
        out_specs=pl.BlockSpec((d, tm), lambda i: (0, i)),
        out_shape=jax.ShapeDtypeStruct((d, n), _BF16),
        compiler_params=_params(("parallel",)),
        name="glu_t",
    )(h, g, w_in_t, b_in_t)


def _dwconv_t_kernel(at_ref, extra_ref, taps_ref, ct_ref, cextra_ref, *, blocks_per_seq):
    cb, nb, _ = at_ref.shape
    u_idx = lax.broadcasted_iota(jnp.int32, (BLOCK, BLOCK), 0)
    t_idx = lax.broadcasted_iota(jnp.int32, (BLOCK, BLOCK), 1)
    in_block = t_idx >= u_idx
    seq_start = (lax.broadcasted_iota(jnp.int32, (nb, BLOCK), 0) & (blocks_per_seq - 1)) == 0
    for ci in range(cb):
        circ = pltpu.roll(jnp.broadcast_to(taps_ref[ci:ci + 1, :], (BLOCK, BLOCK)), 0, 1, stride=1, stride_axis=0)
        w = jnp.concatenate([jnp.where(in_block, circ, 0.0), jnp.where(in_block, 0.0, circ)], axis=1).astype(_BF16)
        x = jnp.concatenate([at_ref[ci], extra_ref[ci]], axis=0)
        y = _dot(x, w)
        from_prev = jnp.where(seq_start, y[nb:nb + 1, BLOCK:], pltpu.roll(y[:nb, BLOCK:], 1, axis=0))
        ct_ref[ci] = y[:nb, :BLOCK] + from_prev
        cextra_ref[ci] = y[nb:, :BLOCK]


def _dwconv_t(at, extra, taps, *, blocks_per_seq):
    d, n = at.shape
    nb = n // BLOCK
    cb = CONV_CHANNELS
    chan = lambda i: (i, 0, 0)
    ct, cextra = pl.pallas_call(
        functools.partial(_dwconv_t_kernel, blocks_per_seq=blocks_per_seq),
        grid=(d // cb,),
        in_specs=[pl.BlockSpec((cb, nb, BLOCK), chan), pl.BlockSpec((cb, EXTRA_ROWS, BLOCK), chan),
                  pl.BlockSpec((cb, BLOCK), lambda i: (i, 0))],
        out_specs=[pl.BlockSpec((cb, nb, BLOCK), chan), pl.BlockSpec((cb, EXTRA_ROWS, BLOCK), chan)],
        out_shape=[jax.ShapeDtypeStruct((d, nb, BLOCK), _F32), jax.ShapeDtypeStruct((d, EXTRA_ROWS, BLOCK), _F32)],
        compiler_params=_params(("parallel",)),
        name="dwconv_t",
    )(at.reshape(d, nb, BLOCK), extra, taps)
    return ct.reshape(d, n), cextra


def _head_pair_rms(x, gain_pair, scale):
    low = lax.broadcasted_iota(jnp.int32, x.shape, 1) < HEAD_DIM
    sq = x * x
    ms_lo = jnp.sum(jnp.where(low, sq, 0.0), axis=-1, keepdims=True) * (1.0 / HEAD_DIM)
    ms_hi = jnp.sum(jnp.where(low, 0.0, sq), axis=-1, keepdims=True) * (1.0 / HEAD_DIM)
    inv = jnp.where(low, lax.rsqrt(ms_lo + NORM_EPS), lax.rsqrt(ms_hi + NORM_EPS))
    return x * inv * gain_pair * scale


def _ffn_kernel(*refs, pre_conv, pre_oproj, post_q, post_kv):
    refs = list(refs)
    h_ref = refs.pop(0)
    h = h_ref[...]
    if pre_conv:
        ct_ref, lng_ref, lnb_ref, woutt_ref, bout_ref = (refs.pop(0) for _ in range(5))
        c = ct_ref[...]
        mu = jnp.mean(c, axis=0, keepdims=True)
        cen = c - mu
        var = jnp.mean(cen * cen, axis=0, keepdims=True)
        y = _silu(cen * lax.rsqrt(var + NORM_EPS) * lng_ref[...] + lnb_ref[...]).astype(_BF16)
        h = h + _dot(woutt_ref[...], y).T + bout_ref[...]
    if pre_oproj:
        a_ref, wo_ref = refs.pop(0), refs.pop(0)
        h = h + _dot(a_ref[...], wo_ref[...])
    g_ref, wg_ref, wu_ref, wd_ref = (refs.pop(0) for _ in range(4))
    if post_q:
        gq_ref, wq_ref, qg_ref = (refs.pop(0) for _ in range(3))
    if post_kv:
        gkv_ref, wk_ref, wvt_ref, kg_ref = (refs.pop(0) for _ in range(4))
    o_ref = refs.pop(0)

    u = _rms(h, g_ref[...]).astype(_BF16)
    z = (_silu(_dot(u, wg_ref[...])) * _dot(u, wu_ref[...])).astype(_BF16)
    h = h + _dot(z, wd_ref[...])
    o_ref[...] = h

    if post_q:
        q_ref = refs.pop(0)
        q = _dot(_rms(h, gq_ref[...]).astype(_BF16), wq_ref[...])
        for c in range(q_ref.shape[1] // LANES):
            cl = slice(c * LANES, (c + 1) * LANES)
            q_ref[:, cl] = _head_pair_rms(q[:, cl], qg_ref[...], HEAD_DIM ** -0.5).astype(_BF16)
    if post_kv:
        k_ref, vt_ref = refs.pop(0), refs.pop(0)
        ukv = _rms(h, gkv_ref[...]).astype(_BF16)
        k = _dot(ukv, wk_ref[...])
        for c in range(k_ref.shape[1] // LANES):
            cl = slice(c * LANES, (c + 1) * LANES)
            k_ref[:, cl] = _rms(k[:, cl], kg_ref[...]).astype(_BF16)
        vt_ref[...] = _dot_nt(wvt_ref[...], ukv).astype(_BF16)


def _ffn(h, ffn_w, *, conv=None, oproj=None, q_w=None, kv_w=None):
    n, d = h.shape
    f = ffn_w[1].shape[1]
    tm = min(TOKEN_TILE, n)
    row = lambda i: (i, 0)
    args, in_specs = [h], [pl.BlockSpec((tm, d), row)]
    if conv is not None:
        args += list(conv)
        in_specs += [pl.BlockSpec((d, tm), lambda i: (0, i))] + [_const_spec(a.shape) for a in conv[1:]]
    if oproj is not None:
        args += list(oproj)
        in_specs += [pl.BlockSpec((tm, oproj[0].shape[1]), row), _const_spec(oproj[1].shape)]
    args += list(ffn_w)
    in_specs += [_const_spec((1, d)), _const_spec((d, f)), _const_spec((d, f)), _const_spec((f, d))]
    out_specs = [pl.BlockSpec((tm, d), row)]
    out_shape = [jax.ShapeDtypeStruct((n, d), _F32)]
    if q_w is not None:
        qd = q_w[1].shape[1]
        args += list(q_w)
        in_specs += [_const_spec(a.shape) for a in q_w]
        out_specs.append(pl.BlockSpec((tm, qd), row))
        out_shape.append(jax.ShapeDtypeStruct((n, qd), _BF16))
    if kv_w is not None:
        kd = kv_w[1].shape[1]
        args += list(kv_w)
        in_specs += [_const_spec(a.shape) for a in kv_w]
        out_specs += [pl.BlockSpec((tm, kd), row), pl.BlockSpec((kd, tm), lambda i: (0, i))]
        out_shape += [jax.ShapeDtypeStruct((n, kd), _BF16), jax.ShapeDtypeStruct((kd, n), _BF16)]
    return pl.pallas_call(
        functools.partial(_ffn_kernel, pre_conv=conv is not None, pre_oproj=oproj is not None,
                          post_q=q_w is not None, post_kv=kv_w is not None),
        grid=(n // tm,),
        in_specs=in_specs, out_specs=out_specs, out_shape=out_shape,
        compiler_params=_params(("parallel",)),
        name="ffn",
    )(*args)


def _attn_kernel(sink_ref, q_ref, kp_ref, kc_ref, km_ref, vtp_ref, vtc_ref, vtm_ref, o_ref):
    n = pl.program_id(1)
    nq = Q_PER_KV * BLOCK
    low = lax.broadcasted_iota(jnp.int32, (BLOCK, LANES), 1) < HEAD_DIM
    row_low = lax.broadcasted_iota(jnp.int32, (LANES, BLOCK), 0) < HEAD_DIM

    c = lax.broadcasted_iota(jnp.int32, (BLOCK, nq), 0)
    r = lax.broadcasted_iota(jnp.int32, (BLOCK, nq), 1) & (BLOCK - 1)
    use_cur = c <= r
    band_ok = use_cur | (n > 0)
    zero_bf = jnp.zeros((BLOCK, LANES), _BF16)
    meta_pad = jnp.zeros((BLOCK - N_META, nq), _BF16)

    for g in range(N_KV_HEADS):
        gl = slice(g * LANES, (g + 1) * LANES)
        parts = []
        for p in range(Q_PER_KV // 2):
            qp = q_ref[:, (2 * g + p) * LANES:(2 * g + p + 1) * LANES]
            parts += [jnp.where(low, qp, zero_bf), jnp.where(low, zero_bf, qp)]
        qs = jnp.concatenate(parts, axis=0)
        kb = jnp.concatenate([kp_ref[:, gl], kc_ref[:, gl]], axis=0)
        st = _dot_nt(kb, qs)
        s = jnp.where(band_ok, jnp.where(use_cur, st[BLOCK:], st[:BLOCK]), NEG_INF)
        sm = _dot_nt(km_ref[:, gl], qs)
        sink = jnp.concatenate(
            [jnp.full((1, BLOCK), sink_ref[g * Q_PER_KV + hh], _F32) for hh in range(Q_PER_KV)], axis=1)
        mx = jnp.maximum(jnp.maximum(jnp.max(s, axis=0, keepdims=True),
                                     jnp.max(sm, axis=0, keepdims=True)), sink)
        pe = jnp.exp(s - mx)
        pme = jnp.exp(sm - mx)
        denom = jnp.sum(pe, axis=0, keepdims=True) + jnp.sum(pme, axis=0, keepdims=True) + jnp.exp(sink - mx)
        pt = jnp.concatenate([jnp.where(use_cur, 0.0, pe).astype(_BF16), jnp.where(use_cur, pe, 0.0).astype(_BF16),
                              pme.astype(_BF16), meta_pad], axis=0)
        vt = jnp.concatenate([vtp_ref[gl, :], vtc_ref[gl, :], vtm_ref[gl, :]], axis=1)
        ot = _dot(vt, pt) / denom
        for p in range(Q_PER_KV // 2):
            pair_t = jnp.where(row_low, ot[:, 2 * p * BLOCK:(2 * p + 1) * BLOCK],
                               ot[:, (2 * p + 1) * BLOCK:(2 * p + 2) * BLOCK])
            o_ref[:, (2 * g + p) * LANES:(2 * g + p + 1) * LANES] = pair_t.T.astype(_BF16)


def _attention(sinks, q, k, vt, k_meta, vt_meta, *, batch, n_blocks):
    n, qd = q.shape
    kd = k.shape[1]
    cur = lambda b, j: (b * n_blocks + j, 0)
    prev = lambda b, j: (b * n_blocks + jnp.maximum(j - 1, 0), 0)
    cur_t = lambda b, j: (0, b * n_blocks + j)
    prev_t = lambda b, j: (0, b * n_blocks + jnp.maximum(j - 1, 0))
    return pl.pallas_call(
        _attn_kernel,
        grid=(batch, n_blocks),
        in_specs=[
            pl.BlockSpec(memory_space=pltpu.SMEM),
            pl.BlockSpec((BLOCK, qd), cur),
            pl.BlockSpec((BLOCK, kd), prev), pl.BlockSpec((BLOCK, kd), cur), _const_spec((N_META, kd)),
            pl.BlockSpec((kd, BLOCK), prev_t), pl.BlockSpec((kd, BLOCK), cur_t), _const_spec((kd, BLOCK)),
        ],
        out_specs=pl.BlockSpec((BLOCK, qd), cur),
        out_shape=jax.ShapeDtypeStruct((n, qd), _BF16),
        compiler_params=_params(("parallel", "parallel")),
        name="swa_attention",
    )(sinks, q, k, k, k_meta, vt, vt, vt_meta)


def _dup_heads(w):
    lead = w.shape[:-1]
    w = w.reshape(lead + (N_KV_HEADS, 1, HEAD_DIM))
    return jnp.broadcast_to(w, lead + (N_KV_HEADS, 2, HEAD_DIM)).reshape(lead + (N_KV_HEADS * 2 * HEAD_DIM,))


def kernel(x, meta_tokens, norm_mix, norm_ffn, conv_w_in, conv_b_in, conv_dw, conv_ln_g, conv_ln_b,
           conv_w_out, conv_b_out, kv_norm, w_kv, k_norm, w_q, q_norm, attn_sinks, w_o,
           ffn_w_gate, ffn_w_up, ffn_w_down):
    b, seq, d = x.shape
    depth = norm_mix.shape[0]
    n_a = conv_w_in.shape[0]
    assert seq % TOKEN_TILE == 0 and TOKEN_TILE % BLOCK == 0 and n_a >= 1 and depth > n_a
    assert CONV_WIDTH - 1 <= BLOCK and N_META <= BLOCK and d % CONV_CHANNELS == 0
    assert (seq // BLOCK) & (seq // BLOCK - 1) == 0

    row = lambda v: v.reshape(1, -1).astype(_F32)
    col = lambda v: v.reshape(-1, 1).astype(_F32)
    kvd = w_kv.shape[1] // 2
    kv_w = (row(kv_norm), _dup_heads(w_kv[:, :kvd]).astype(_BF16), _dup_heads(w_kv[:, kvd:]).T.astype(_BF16),
            jnp.tile(k_norm.astype(_F32), 2).reshape(1, LANES))
    q_ws = [(row(norm_mix[n_a + j]), w_q[j].astype(_BF16), jnp.tile(q_norm[j].astype(_F32), 2).reshape(1, LANES))
            for j in range(depth - n_a)]
    ffn_ws = [(row(norm_ffn[i]), ffn_w_gate[i].astype(_BF16), ffn_w_up[i].astype(_BF16),
               ffn_w_down[i].astype(_BF16)) for i in range(depth)]

    hm = jnp.pad(meta_tokens.astype(_F32), ((0, BLOCK - N_META), (0, 0)))
    h = x.reshape(b * seq, d)
    k_meta = vt_meta = k_sh = vt_sh = q = None
    for i in range(n_a):
        glu_w = (row(norm_mix[i]), conv_w_in[i].T.astype(_BF16), col(conv_b_in[i]))
        taps = jnp.pad(conv_dw[i].astype(_F32)[::-1].T, ((0, 0), (0, BLOCK - CONV_WIDTH)))
        conv_tail = (col(conv_ln_g[i]), col(conv_ln_b[i]), conv_w_out[i].T.astype(_BF16), row(conv_b_out[i]))
        at_m = _glu_t(hm, *glu_w)
        at = _glu_t(h, *glu_w)
        history = jnp.pad(at_m[:, :N_META], ((0, 0), (BLOCK - N_META, 0)))
        extra = jnp.pad(jnp.stack([history, at_m], axis=1), ((0, 0), (0, EXTRA_ROWS - 2), (0, 0)))
        ct, cextra = _dwconv_t(at, extra, taps, blocks_per_seq=seq // BLOCK)
        conv_m = (cextra[:, 1, :],) + conv_tail
        conv_r = (ct,) + conv_tail
        if i == n_a - 1:
            _, _, k_meta, vt_meta = _ffn(hm, ffn_ws[i], conv=conv_m, q_w=q_ws[0], kv_w=kv_w)
            h, q, k_sh, vt_sh = _ffn(h, ffn_ws[i], conv=conv_r, q_w=q_ws[0], kv_w=kv_w)
        else:
            hm, = _ffn(hm, ffn_ws[i], conv=conv_m)
            h, = _ffn(h, ffn_ws[i], conv=conv_r)
    k_meta = k_meta[:N_META]
    vt_meta = jnp.where(lax.broadcasted_iota(jnp.int32, vt_meta.shape, 1) < N_META, vt_meta, 0)
    for j in range(depth - n_a):
        layer = n_a + j
        a = _attention(attn_sinks[j].astype(_F32), q, k_sh, vt_sh, k_meta, vt_meta, batch=b, n_blocks=seq // BLOCK)
        oproj = (a, w_o[j].astype(_BF16))
        if layer + 1 < depth:
            h, q = _ffn(h, ffn_ws[layer], oproj=oproj, q_w=q_ws[j + 1])
        else:
            h, = _ffn(h, ffn_ws[layer], oproj=oproj)
    return h.reshape(b, seq, d)
```

```python
import functools

import jax
import jax.numpy as jnp
from jax import lax
from jax.experimental import pallas as pl
from jax.experimental.pallas import tpu as pltpu

N_META = 16
CONV_WIDTH = 31
HEAD_DIM = 64
N_KV_HEADS = 4
Q_PER_KV = 4
BLOCK = 128
NORM_EPS = 1e-6
NEG_INF = -1e30

LANES = 128
CONV_CHANNELS = 16
EXTRA_ROWS = 16
STAGE_PITCH = 24
TOKEN_TILE = 512
ATTN_BLOCKS = 4
VMEM_LIMIT = 56 * 1024 * 1024

_BF16 = jnp.bfloat16
_F32 = jnp.float32


def _const_spec(shape):
    return pl.BlockSpec(shape, lambda *_: (0,) * len(shape), pipeline_mode=pl.Buffered(1))


def _params(semantics):
    return pltpu.CompilerParams(dimension_semantics=semantics, vmem_limit_bytes=VMEM_LIMIT)


def _rms(x, g):
    ms = jnp.mean(x * x, axis=-1, keepdims=True)
    return x * lax.rsqrt(ms + NORM_EPS) * g


def _silu(x):
    return x * jax.nn.sigmoid(x)


def _dot(a, b):
    return jnp.dot(a, b, preferred_element_type=_F32)


def _dot_nt(a, b):
    return lax.dot_general(a, b, (((1,), (1,)), ((), ())), preferred_element_type=_F32)


def _glu_t_kernel(h_ref, g_ref, wint_ref, bint_ref, at_ref):
    d = h_ref.shape[1]
    u = _rms(h_ref[...], g_ref[...]).astype(_BF16)
    a2t = _dot_nt(wint_ref[...], u) + bint_ref[...]
    at_ref[...] = (a2t[:d] * jax.nn.sigmoid(a2t[d:])).astype(_BF16)


def _glu_t(h, g, w_in_t, b_in_t):
    n, d = h.shape
    tm = min(TOKEN_TILE, n)
    return pl.pallas_call(
        _glu_t_kernel,
        grid=(n // tm,),
        in_specs=[pl.BlockSpec((tm, d), lambda i: (i, 0)), _const_spec((1, d)),
                  _const_spec((2 * d, d)), _const_spec((2 * d, 1))],
        out_specs=pl.BlockSpec((d, tm), lambda i: (0, i)),
        out_shape=jax.ShapeDtypeStruct((d, n), _BF16),
        compiler_params=_params(("parallel",)),
        name="glu_t",
    )(h, g, w_in_t, b_in_t)


def _dwconv_t_kernel(at_ref, extra_ref, taps_ref, ct_ref, cextra_ref, stage_ref, *, blocks_per_seq):
    cb, nb, _ = at_ref.shape
    u_idx = lax.broadcasted_iota(jnp.int32, (BLOCK, BLOCK), 0)
    t_idx = lax.broadcasted_iota(jnp.int32, (BLOCK, BLOCK), 1)
    in_block = t_idx >= u_idx
    seq_start = (lax.broadcasted_iota(jnp.int32, (nb, BLOCK), 0) & (blocks_per_seq - 1)) == 0
    for ci in range(cb):
        circ = pltpu.roll(jnp.broadcast_to(taps_ref[ci:ci + 1, :], (BLOCK, BLOCK)), 0, 1, stride=1, stride_axis=0)
        w = jnp.concatenate([jnp.where(in_block, circ, 0.0), jnp.where(in_block, 0.0, circ)], axis=1).astype(_BF16)
        x = jnp.concatenate([at_ref[ci], extra_ref[ci]], axis=0)
        y = _dot(x, w)
        from_prev = jnp.where(seq_start, y[nb:nb + 1, BLOCK:], pltpu.roll(y[:nb, BLOCK:], 1, axis=0))
        stage_ref[pl.ds(ci, nb, stride=STAGE_PITCH), :] = y[:nb, :BLOCK] + from_prev
        cextra_ref[ci] = y[nb:, :BLOCK]
    for blk in range(nb):
        ct_ref[:, blk * BLOCK:(blk + 1) * BLOCK] = stage_ref[blk * STAGE_PITCH:blk * STAGE_PITCH + cb, :]


def _dwconv_t(at, extra, taps, *, blocks_per_seq):
    d, n = at.shape
    nb = n // BLOCK
    cb = CONV_CHANNELS
    chan = lambda i: (i, 0, 0)
    return pl.pallas_call(
        functools.partial(_dwconv_t_kernel, blocks_per_seq=blocks_per_seq),
        grid=(d // cb,),
        in_specs=[pl.BlockSpec((cb, nb, BLOCK), chan), pl.BlockSpec((cb, EXTRA_ROWS, BLOCK), chan),
                  pl.BlockSpec((cb, BLOCK), lambda i: (i, 0))],
        out_specs=[pl.BlockSpec((cb, n), lambda i: (i, 0)), pl.BlockSpec((cb, EXTRA_ROWS, BLOCK), chan)],
        out_shape=[jax.ShapeDtypeStruct((d, n), _F32), jax.ShapeDtypeStruct((d, EXTRA_ROWS, BLOCK), _F32)],
        scratch_shapes=[pltpu.VMEM((nb * STAGE_PITCH, BLOCK), _F32)],
        compiler_params=_params(("parallel",)),
        name="dwconv_t",
    )(at.reshape(d, nb, BLOCK), extra, taps)


def _head_pair_rms(x, gain_pair, scale):
    low = lax.broadcasted_iota(jnp.int32, x.shape, 1) < HEAD_DIM
    sq = x * x
    ms_lo = jnp.sum(jnp.where(low, sq, 0.0), axis=-1, keepdims=True) * (1.0 / HEAD_DIM)
    ms_hi = jnp.sum(jnp.where(low, 0.0, sq), axis=-1, keepdims=True) * (1.0 / HEAD_DIM)
    inv = jnp.where(low, lax.rsqrt(ms_lo + NORM_EPS), lax.rsqrt(ms_hi + NORM_EPS))
    return x * inv * gain_pair * scale


def _ffn_kernel(*refs, pre_conv, pre_oproj, post_q, post_kv):
    refs = list(refs)
    h_ref = refs.pop(0)
    h = h_ref[...]
    if pre_conv:
        ct_ref, lng_ref, lnb_ref, woutt_ref, bout_ref = (refs.pop(0) for _ in range(5))
        c = ct_ref[...]
        mu = jnp.mean(c, axis=0, keepdims=True)
        cen = c - mu
        var = jnp.mean(cen * cen, axis=0, keepdims=True)
        y = _silu(cen * lax.rsqrt(var + NORM_EPS) * lng_ref[...] + lnb_ref[...]).astype(_BF16)
        h = h + lax.dot_general(y, woutt_ref[...], (((0,), (1,)), ((), ())), preferred_element_type=_F32) + bout_ref[...]
    if pre_oproj:
        a_ref, wo_ref = refs.pop(0), refs.pop(0)
        h = h + _dot(a_ref[...], wo_ref[...])
    g_ref, wg_ref, wu_ref, wd_ref = (refs.pop(0) for _ in range(4))
    if post_q:
        gq_ref, wq_ref, qg_ref = (refs.pop(0) for _ in range(3))
    if post_kv:
        gkv_ref, wk_ref, wvt_ref, kg_ref = (refs.pop(0) for _ in range(4))
    o_ref = refs.pop(0)

    u = _rms(h, g_ref[...]).astype(_BF16)
    z = (_silu(_dot(u, wg_ref[...])) * _dot(u, wu_ref[...])).astype(_BF16)
    h = h + _dot(z, wd_ref[...])
    o_ref[...] = h

    if post_q:
        q_ref = refs.pop(0)
        q = _dot(_rms(h, gq_ref[...]).astype(_BF16), wq_ref[...])
        for c in range(q_ref.shape[1] // LANES):
            cl = slice(c * LANES, (c + 1) * LANES)
            q_ref[:, cl] = _head_pair_rms(q[:, cl], qg_ref[...], HEAD_DIM ** -0.5).astype(_BF16)
    if post_kv:
        k_ref, vt_ref = refs.pop(0), refs.pop(0)
        ukv = _rms(h, gkv_ref[...]).astype(_BF16)
        k = _dot(ukv, wk_ref[...])
        for c in range(k_ref.shape[1] // LANES):
            cl = slice(c * LANES, (c + 1) * LANES)
            k_ref[:, cl] = _rms(k[:, cl], kg_ref[...]).astype(_BF16)
        vt_ref[...] = _dot_nt(wvt_ref[...], ukv).astype(_BF16)


def _ffn(h, ffn_w, *, conv=None, oproj=None, q_w=None, kv_w=None):
    n, d = h.shape
    f = ffn_w[1].shape[1]
    tm = min(TOKEN_TILE, n)
    row = lambda i: (i, 0)
    args, in_specs = [h], [pl.BlockSpec((tm, d), row)]
    if conv is not None:
        args += list(conv)
        in_specs += [pl.BlockSpec((d, tm), lambda i: (0, i))] + [_const_spec(a.shape) for a in conv[1:]]
    if oproj is not None:
        args += list(oproj)
        in_specs += [pl.BlockSpec((tm, oproj[0].shape[1]), row), _const_spec(oproj[1].shape)]
    args += list(ffn_w)
    in_specs += [_const_spec((1, d)), _const_spec((d, f)), _const_spec((d, f)), _const_spec((f, d))]
    out_specs = [pl.BlockSpec((tm, d), row)]
    out_shape = [jax.ShapeDtypeStruct((n, d), _F32)]
    if q_w is not None:
        qd = q_w[1].shape[1]
        args += list(q_w)
        in_specs += [_const_spec(a.shape) for a in q_w]
        out_specs.append(pl.BlockSpec((tm, qd), row))
        out_shape.append(jax.ShapeDtypeStruct((n, qd), _BF16))
    if kv_w is not None:
        kd = kv_w[1].shape[1]
        args += list(kv_w)
        in_specs += [_const_spec(a.shape) for a in kv_w]
        out_specs += [pl.BlockSpec((tm, kd), row), pl.BlockSpec((kd, tm), lambda i: (0, i))]
        out_shape += [jax.ShapeDtypeStruct((n, kd), _BF16), jax.ShapeDtypeStruct((kd, n), _BF16)]
    return pl.pallas_call(
        functools.partial(_ffn_kernel, pre_conv=conv is not None, pre_oproj=oproj is not None,
                          post_q=q_w is not None, post_kv=kv_w is not None),
        grid=(n // tm,),
        in_specs=in_specs, out_specs=out_specs, out_shape=out_shape,
        compiler_params=_params(("parallel",)),
        name="ffn",
    )(*args)


def _attend_block(sink_ref, q, k_prev, k_cur, k_meta, vt_prev, vt_cur, vt_meta, prev_bias):
    nq = Q_PER_KV * BLOCK
    low = lax.broadcasted_iota(jnp.int32, (BLOCK, LANES), 1) < HEAD_DIM
    row_low = lax.broadcasted_iota(jnp.int32, (LANES, BLOCK), 0) < HEAD_DIM
    c = lax.broadcasted_iota(jnp.int32, (BLOCK, nq), 0)
    r = lax.broadcasted_iota(jnp.int32, (BLOCK, nq), 1) & (BLOCK - 1)
    use_cur = c <= r
    zero_bf = jnp.zeros((BLOCK, LANES), _BF16)
    meta_pad = jnp.zeros((BLOCK - N_META, nq), _BF16)
    out = []
    for g in range(N_KV_HEADS):
        gl = slice(g * LANES, (g + 1) * LANES)
        parts = []
        for p in range(Q_PER_KV // 2):
            qp = q[:, (2 * g + p) * LANES:(2 * g + p + 1) * LANES]
            parts += [jnp.where(low, qp, zero_bf), jnp.where(low, zero_bf, qp)]
        qs = jnp.concatenate(parts, axis=0)
        kb = jnp.concatenate([k_prev[:, gl], k_cur[:, gl]], axis=0)
        st = _dot_nt(kb, qs)
        s = jnp.where(use_cur, st[BLOCK:], st[:BLOCK] + prev_bias)
        sm = _dot_nt(k_meta[:, gl], qs)
        sink = jnp.concatenate(
            [jnp.full((1, BLOCK), sink_ref[g * Q_PER_KV + hh], _F32) for hh in range(Q_PER_KV)], axis=1)
        mx = jnp.maximum(jnp.maximum(jnp.max(s, axis=0, keepdims=True),
                                     jnp.max(sm, axis=0, keepdims=True)), sink)
        pe = jnp.exp(s - mx)
        pme = jnp.exp(sm - mx)
        denom = jnp.sum(pe, axis=0, keepdims=True) + jnp.sum(pme, axis=0, keepdims=True) + jnp.exp(sink - mx)
        p_cur = jnp.where(use_cur, pe, 0.0)
        pt = jnp.concatenate([(pe - p_cur).astype(_BF16), p_cur.astype(_BF16), pme.astype(_BF16), meta_pad],
                             axis=0)
        vt = jnp.concatenate([vt_prev[gl, :], vt_cur[gl, :], vt_meta[gl, :]], axis=1)
        ot = _dot(vt, pt) / denom
        for p in range(Q_PER_KV // 2):
            pair_t = jnp.where(row_low, ot[:, 2 * p * BLOCK:(2 * p + 1) * BLOCK],
                               ot[:, (2 * p + 1) * BLOCK:(2 * p + 2) * BLOCK])
            out.append(pair_t.T.astype(_BF16))
    return jnp.concatenate(out, axis=1)


def _attn_kernel(sink_ref, q_ref, kp_ref, kc_ref, km_ref, vtp_ref, vtc_ref, vtm_ref, o_ref):
    first_bias = jnp.where(pl.program_id(1) > 0, 0.0, NEG_INF)
    for sub in range(ATTN_BLOCKS):
        rows = slice(sub * BLOCK, (sub + 1) * BLOCK)
        if sub == 0:
            k_prev, vt_prev, bias = kp_ref[...], vtp_ref[...], first_bias
        else:
            before = slice((sub - 1) * BLOCK, sub * BLOCK)
            k_prev, vt_prev, bias = kc_ref[before, :], vtc_ref[:, before], 0.0
        o_ref[rows, :] = _attend_block(sink_ref, q_ref[rows, :], k_prev, kc_ref[rows, :], km_ref[...],
                                       vt_prev, vtc_ref[:, rows], vtm_ref[...], bias)


def _attention(sinks, q, k, vt, k_meta, vt_meta, *, batch, n_blocks):
    n, qd = q.shape
    kd = k.shape[1]
    steps = n_blocks // ATTN_BLOCKS
    rows = ATTN_BLOCKS * BLOCK
    cur = lambda b, j: (b * steps + j, 0)
    prev = lambda b, j: (b * n_blocks + jnp.maximum(j * ATTN_BLOCKS - 1, 0), 0)
    cur_t = lambda b, j: (0, b * steps + j)
    prev_t = lambda b, j: (0, b * n_blocks + jnp.maximum(j * ATTN_BLOCKS - 1, 0))
    return pl.pallas_call(
        _attn_kernel,
        grid=(batch, steps),
        in_specs=[
            pl.BlockSpec(memory_space=pltpu.SMEM),
            pl.BlockSpec((rows, qd), cur),
            pl.BlockSpec((BLOCK, kd), prev), pl.BlockSpec((rows, kd), cur), _const_spec((N_META, kd)),
            pl.BlockSpec((kd, BLOCK), prev_t), pl.BlockSpec((kd, rows), cur_t), _const_spec((kd, BLOCK)),
        ],
        out_specs=pl.BlockSpec((rows, qd), cur),
        out_shape=jax.ShapeDtypeStruct((n, qd), _BF16),
        compiler_params=_params(("parallel", "parallel")),
        name="swa_attention",
    )(sinks, q, k, k, k_meta, vt, vt, vt_meta)


def _dup_heads(w):
    lead = w.shape[:-1]
    w = w.reshape(lead + (N_KV_HEADS, 1, HEAD_DIM))
    return jnp.broadcast_to(w, lead + (N_KV_HEADS, 2, HEAD_DIM)).reshape(lead + (N_KV_HEADS * 2 * HEAD_DIM,))


def kernel(x, meta_tokens, norm_mix, norm_ffn, conv_w_in, conv_b_in, conv_dw, conv_ln_g, conv_ln_b,
           conv_w_out, conv_b_out, kv_norm, w_kv, k_norm, w_q, q_norm, attn_sinks, w_o,
           ffn_w_gate, ffn_w_up, ffn_w_down):
    b, seq, d = x.shape
    depth = norm_mix.shape[0]
    n_a = conv_w_in.shape[0]
    assert seq % TOKEN_TILE == 0 and TOKEN_TILE % BLOCK == 0 and n_a >= 1 and depth > n_a
    assert CONV_WIDTH - 1 <= BLOCK and N_META <= BLOCK and d % CONV_CHANNELS == 0
    assert (seq // BLOCK) & (seq // BLOCK - 1) == 0 and (seq // BLOCK) % ATTN_BLOCKS == 0

    row = lambda v: v.reshape(1, -1).astype(_F32)
    col = lambda v: v.reshape(-1, 1).astype(_F32)
    kvd = w_kv.shape[1] // 2
    kv_w = (row(kv_norm), _dup_heads(w_kv[:, :kvd]).astype(_BF16), _dup_heads(w_kv[:, kvd:]).T.astype(_BF16),
            jnp.tile(k_norm.astype(_F32), 2).reshape(1, LANES))
    q_ws = [(row(norm_mix[n_a + j]), w_q[j].astype(_BF16), jnp.tile(q_norm[j].astype(_F32), 2).reshape(1, LANES))
            for j in range(depth - n_a)]
    ffn_ws = [(row(norm_ffn[i]), ffn_w_gate[i].astype(_BF16), ffn_w_up[i].astype(_BF16),
               ffn_w_down[i].astype(_BF16)) for i in range(depth)]

    hm = jnp.pad(meta_tokens.astype(_F32), ((0, BLOCK - N_META), (0, 0)))
    h = x.reshape(b * seq, d)
    k_meta = vt_meta = k_sh = vt_sh = q = None
    for i in range(n_a):
        glu_w = (row(norm_mix[i]), conv_w_in[i].T.astype(_BF16), col(conv_b_in[i]))
        taps = jnp.pad(conv_dw[i].astype(_F32)[::-1].T, ((0, 0), (0, BLOCK - CONV_WIDTH)))
        conv_tail = (col(conv_ln_g[i]), col(conv_ln_b[i]), conv_w_out[i].T.astype(_BF16), row(conv_b_out[i]))
        at_m = _glu_t(hm, *glu_w)
        at = _glu_t(h, *glu_w)
        history = jnp.pad(at_m[:, :N_META], ((0, 0), (BLOCK - N_META, 0)))
        extra = jnp.pad(jnp.stack([history, at_m], axis=1), ((0, 0), (0, EXTRA_ROWS - 2), (0, 0)))
        ct, cextra = _dwconv_t(at, extra, taps, blocks_per_seq=seq // BLOCK)
        conv_m = (cextra[:, 1, :],) + conv_tail
        conv_r = (ct,) + conv_tail
        if i == n_a - 1:
            _, _, k_meta, vt_meta = _ffn(hm, ffn_ws[i], conv=conv_m, q_w=q_ws[0], kv_w=kv_w)
            h, q, k_sh, vt_sh = _ffn(h, ffn_ws[i], conv=conv_r, q_w=q_ws[0], kv_w=kv_w)
        else:
            hm, = _ffn(hm, ffn_ws[i], conv=conv_m)
            h, = _ffn(h, ffn_ws[i], conv=conv_r)
    k_meta = k_meta[:N_META]
    vt_meta = jnp.where(lax.broadcasted_iota(jnp.int32, vt_meta.shape, 1) < N_META, vt_meta, 0)
    for j in range(depth - n_a):
        layer = n_a + j
        a = _attention(attn_sinks[j].astype(_F32), q, k_sh, vt_sh, k_meta, vt_meta, batch=b, n_blocks=seq // BLOCK)
        oproj = (a, w_o[j].astype(_BF16))
        if layer + 1 < depth:
            h, q = _ffn(h, ffn_ws[layer], oproj=oproj, q_w=q_ws[j + 1])
        else:
            h, = _ffn(h, ffn_ws[layer], oproj=oproj)
    return h.reshape(b, seq, d)
```

```python
import functools

import jax
import jax.numpy as jnp
from jax import lax
from jax.experimental import pallas as pl
from jax.experimental.pallas import tpu as pltpu

N_META = 16
CONV_WIDTH = 31
HEAD_DIM = 64
N_KV_HEADS = 4
Q_PER_KV = 4
BLOCK = 128
NORM_EPS = 1e-6
NEG_INF = -1e30

LANES = 128
CONV_CHANNELS = 16
EXTRA_ROWS = 16
STAGE_PITCH = 24
TOKEN_TILE = 512
ATTN_BLOCKS = 4
VMEM_LIMIT = 56 * 1024 * 1024

_BF16 = jnp.bfloat16
_F32 = jnp.float32


def _const_spec(shape):
    return pl.BlockSpec(shape, lambda *_: (0,) * len(shape), pipeline_mode=pl.Buffered(1))


def _params(semantics):
    return pltpu.CompilerParams(dimension_semantics=semantics, vmem_limit_bytes=VMEM_LIMIT)


def _rms(x, g):
    ms = jnp.mean(x * x, axis=-1, keepdims=True)
    return x * lax.rsqrt(ms + NORM_EPS) * g


def _silu(x):
    return x * jax.nn.sigmoid(x)


def _dot(a, b):
    return jnp.dot(a, b, preferred_element_type=_F32)


def _dot_nt(a, b):
    return lax.dot_general(a, b, (((1,), (1,)), ((), ())), preferred_element_type=_F32)


def _glu_t_kernel(h_ref, g_ref, wint_ref, bint_ref, at_ref):
    d = h_ref.shape[1]
    u = _rms(h_ref[...], g_ref[...]).astype(_BF16)
    a2t = _dot_nt(wint_ref[...], u) + bint_ref[...]
    at_ref[...] = (a2t[:d] * jax.nn.sigmoid(a2t[d:])).astype(_BF16)


def _glu_t(h, g, w_in_t, b_in_t):
    n, d = h.shape
    tm = min(TOKEN_TILE, n)
    return pl.pallas_call(
        _glu_t_kernel,
        grid=(n // tm,),
        in_specs=[pl.BlockSpec((tm, d), lambda i: (i, 0)), _const_spec((1, d)),
                  _const_spec((2 * d, d)), _const_spec((2 * d, 1))],
        out_specs=pl.BlockSpec((d, tm), lambda i: (0, i)),
        out_shape=jax.ShapeDtypeStruct((d, n), _BF16),
        compiler_params=_params(("parallel",)),
        name="glu_t",
    )(h, g, w_in_t, b_in_t)


def _dwconv_t_kernel(at_ref, extra_ref, taps_ref, ct_ref, cextra_ref, stage_ref, *, blocks_per_seq):
    cb, nb, _ = at_ref.shape
    u_idx = lax.broadcasted_iota(jnp.int32, (BLOCK, BLOCK), 0)
    t_idx = lax.broadcasted_iota(jnp.int32, (BLOCK, BLOCK), 1)
    in_block = t_idx >= u_idx
    seq_start = (lax.broadcasted_iota(jnp.int32, (nb, BLOCK), 0) & (blocks_per_seq - 1)) == 0
    for ci in range(cb):
        circ = pltpu.roll(jnp.broadcast_to(taps_ref[ci:ci + 1, :], (BLOCK, BLOCK)), 0, 1, stride=1, stride_axis=0)
        w = jnp.concatenate([jnp.where(in_block, circ, 0.0), jnp.where(in_block, 0.0, circ)], axis=1).astype(_BF16)
        x = jnp.concatenate([at_ref[ci], extra_ref[ci]], axis=0)
        y = _dot(x, w)
        from_prev = jnp.where(seq_start, y[nb:nb + 1, BLOCK:], pltpu.roll(y[:nb, BLOCK:], 1, axis=0))
        stage_ref[pl.ds(ci, nb, stride=STAGE_PITCH), :] = y[:nb, :BLOCK] + from_prev
        cextra_ref[ci] = y[nb:, :BLOCK]
    for blk in range(nb):
        ct_ref[:, blk * BLOCK:(blk + 1) * BLOCK] = stage_ref[blk * STAGE_PITCH:blk * STAGE_PITCH + cb, :]


def _dwconv_t(at, extra, taps, *, blocks_per_seq):
    d, n = at.shape
    nb = n // BLOCK
    cb = CONV_CHANNELS
    chan = lambda i: (i, 0, 0)
    return pl.pallas_call(
        functools.partial(_dwconv_t_kernel, blocks_per_seq=blocks_per_seq),
        grid=(d // cb,),
        in_specs=[pl.BlockSpec((cb, nb, BLOCK), chan), pl.BlockSpec((cb, EXTRA_ROWS, BLOCK), chan),
                  pl.BlockSpec((cb, BLOCK), lambda i: (i, 0))],
        out_specs=[pl.BlockSpec((cb, n), lambda i: (i, 0)), pl.BlockSpec((cb, EXTRA_ROWS, BLOCK), chan)],
        out_shape=[jax.ShapeDtypeStruct((d, n), _F32), jax.ShapeDtypeStruct((d, EXTRA_ROWS, BLOCK), _F32)],
        scratch_shapes=[pltpu.VMEM((nb * STAGE_PITCH, BLOCK), _F32)],
        compiler_params=_params(("parallel",)),
        name="dwconv_t",
    )(at.reshape(d, nb, BLOCK), extra, taps)


def _head_pair_rms(x, gain_pair, scale):
    low = lax.broadcasted_iota(jnp.int32, x.shape, 1) < HEAD_DIM
    sq = x * x
    ms_lo = jnp.sum(jnp.where(low, sq, 0.0), axis=-1, keepdims=True) * (1.0 / HEAD_DIM)
    ms_hi = jnp.sum(jnp.where(low, 0.0, sq), axis=-1, keepdims=True) * (1.0 / HEAD_DIM)
    inv = jnp.where(low, lax.rsqrt(ms_lo + NORM_EPS), lax.rsqrt(ms_hi + NORM_EPS))
    return x * inv * gain_pair * scale


def _ffn_kernel(*refs, pre_conv, pre_oproj, post_q, post_kv):
    refs = list(refs)
    h_ref = refs.pop(0)
    h = h_ref[...]
    if pre_conv:
        ct_ref, lng_ref, lnb_ref, woutt_ref, bout_ref = (refs.pop(0) for _ in range(5))
        c = ct_ref[...]
        mu = jnp.mean(c, axis=0, keepdims=True)
        cen = c - mu
        var = jnp.mean(cen * cen, axis=0, keepdims=True)
        y = _silu(cen * lax.rsqrt(var + NORM_EPS) * lng_ref[...] + lnb_ref[...]).astype(_BF16)
        h = h + lax.dot_general(y, woutt_ref[...], (((0,), (1,)), ((), ())), preferred_element_type=_F32) + bout_ref[...]
    if pre_oproj:
        a_ref, wo_ref = refs.pop(0), refs.pop(0)
        h = h + _dot(a_ref[...], wo_ref[...])
    g_ref, wg_ref, wu_ref, wd_ref = (refs.pop(0) for _ in range(4))
    if post_q:
        gq_ref, wq_ref, qg_ref = (refs.pop(0) for _ in range(3))
    if post_kv:
        gkv_ref, wk_ref, wvt_ref, kg_ref = (refs.pop(0) for _ in range(4))
    o_ref = refs.pop(0)

    u = _rms(h, g_ref[...]).astype(_BF16)
    z = (_silu(_dot(u, wg_ref[...])) * _dot(u, wu_ref[...])).astype(_BF16)
    h = h + _dot(z, wd_ref[...])
    o_ref[...] = h

    if post_q:
        q_ref = refs.pop(0)
        q = _dot(_rms(h, gq_ref[...]).astype(_BF16), wq_ref[...])
        for c in range(q_ref.shape[1] // LANES):
            cl = slice(c * LANES, (c + 1) * LANES)
            q_ref[:, cl] = _head_pair_rms(q[:, cl], qg_ref[...], HEAD_DIM ** -0.5).astype(_BF16)
    if post_kv:
        k_ref, vt_ref = refs.pop(0), refs.pop(0)
        ukv = _rms(h, gkv_ref[...]).astype(_BF16)
        k = _dot(ukv, wk_ref[...])
        for c in range(k_ref.shape[1] // LANES):
            cl = slice(c * LANES, (c + 1) * LANES)
            k_ref[:, cl] = _rms(k[:, cl], kg_ref[...]).astype(_BF16)
        vt_ref[...] = _dot_nt(wvt_ref[...], ukv).astype(_BF16)


def _ffn(h, ffn_w, *, conv=None, oproj=None, q_w=None, kv_w=None):
    n, d = h.shape
    f = ffn_w[1].shape[1]
    tm = min(TOKEN_TILE, n)
    row = lambda i: (i, 0)
    args, in_specs = [h], [pl.BlockSpec((tm, d), row)]
    if conv is not None:
        args += list(conv)
        in_specs += [pl.BlockSpec((d, tm), lambda i: (0, i))] + [_const_spec(a.shape) for a in conv[1:]]
    if oproj is not None:
        args += list(oproj)
        in_specs += [pl.BlockSpec((tm, oproj[0].shape[1]), row), _const_spec(oproj[1].shape)]
    args += list(ffn_w)
    in_specs += [_const_spec((1, d)), _const_spec((d, f)), _const_spec((d, f)), _const_spec((f, d))]
    out_specs = [pl.BlockSpec((tm, d), row)]
    out_shape = [jax.ShapeDtypeStruct((n, d), _F32)]
    if q_w is not None:
        qd = q_w[1].shape[1]
        args += list(q_w)
        in_specs += [_const_spec(a.shape) for a in q_w]
        out_specs.append(pl.BlockSpec((tm, qd), row))
        out_shape.append(jax.ShapeDtypeStruct((n, qd), _BF16))
    if kv_w is not None:
        kd = kv_w[1].shape[1]
        args += list(kv_w)
        in_specs += [_const_spec(a.shape) for a in kv_w]
        out_specs += [pl.BlockSpec((tm, kd), row), pl.BlockSpec((kd, tm), lambda i: (0, i))]
        out_shape += [jax.ShapeDtypeStruct((n, kd), _BF16), jax.ShapeDtypeStruct((kd, n), _BF16)]
    return pl.pallas_call(
        functools.partial(_ffn_kernel, pre_conv=conv is not None, pre_oproj=oproj is not None,
                          post_q=q_w is not None, post_kv=kv_w is not None),
        grid=(n // tm,),
        in_specs=in_specs, out_specs=out_specs, out_shape=out_shape,
        compiler_params=_params(("parallel",)),
        name="ffn",
    )(*args)


def _block_scores(q, k_prev, k_cur, k_meta):
    low = lax.broadcasted_iota(jnp.int32, (BLOCK, LANES), 1) < HEAD_DIM
    zero_bf = jnp.zeros((BLOCK, LANES), _BF16)
    scores = []
    for g in range(N_KV_HEADS):
        gl = slice(g * LANES, (g + 1) * LANES)
        parts = []
        for p in range(Q_PER_KV // 2):
            qp = q[:, (2 * g + p) * LANES:(2 * g + p + 1) * LANES]
            parts += [jnp.where(low, qp, zero_bf), jnp.where(low, zero_bf, qp)]
        qs = jnp.concatenate(parts, axis=0)
        kb = jnp.concatenate([k_prev[:, gl], k_cur[:, gl]], axis=0)
        scores.append((_dot_nt(kb, qs), _dot_nt(k_meta[:, gl], qs)))
    return scores


def _block_finish(sink_ref, scores, vt_prev, vt_cur, vt_meta, prev_bias):
    nq = Q_PER_KV * BLOCK
    row_low = lax.broadcasted_iota(jnp.int32, (LANES, BLOCK), 0) < HEAD_DIM
    c = lax.broadcasted_iota(jnp.int32, (BLOCK, nq), 0)
    r = lax.broadcasted_iota(jnp.int32, (BLOCK, nq), 1) & (BLOCK - 1)
    use_cur = c <= r
    meta_pad = jnp.zeros((BLOCK - N_META, nq), _BF16)
    out = []
    for g in range(N_KV_HEADS):
        gl = slice(g * LANES, (g + 1) * LANES)
        st, sm = scores[g]
        s = jnp.where(use_cur, st[BLOCK:], st[:BLOCK] + prev_bias)
        sink = jnp.concatenate(
            [jnp.full((1, BLOCK), sink_ref[g * Q_PER_KV + hh], _F32) for hh in range(Q_PER_KV)], axis=1)
        mx = jnp.maximum(jnp.maximum(jnp.max(s, axis=0, keepdims=True),
                                     jnp.max(sm, axis=0, keepdims=True)), sink)
        pe = jnp.exp(s - mx)
        pme = jnp.exp(sm - mx)
        denom = jnp.sum(pe, axis=0, keepdims=True) + jnp.sum(pme, axis=0, keepdims=True) + jnp.exp(sink - mx)
        p_cur = jnp.where(use_cur, pe, 0.0)
        pt = jnp.concatenate([(pe - p_cur).astype(_BF16), p_cur.astype(_BF16), pme.astype(_BF16), meta_pad],
                             axis=0)
        vt = jnp.concatenate([vt_prev[gl, :], vt_cur[gl, :], vt_meta[gl, :]], axis=1)
        ot = _dot(vt, pt) / denom
        for p in range(Q_PER_KV // 2):
            pair_t = jnp.where(row_low, ot[:, 2 * p * BLOCK:(2 * p + 1) * BLOCK],
                               ot[:, (2 * p + 1) * BLOCK:(2 * p + 2) * BLOCK])
            out.append(pair_t.T.astype(_BF16))
    return jnp.concatenate(out, axis=1)


def _attn_kernel(sink_ref, q_ref, kp_ref, kc_ref, km_ref, vtp_ref, vtc_ref, vtm_ref, o_ref):
    first_bias = jnp.where(pl.program_id(1) > 0, 0.0, NEG_INF)
    rows = [slice(sub * BLOCK, (sub + 1) * BLOCK) for sub in range(ATTN_BLOCKS)]

    def scores(sub):
        k_prev = kp_ref[...] if sub == 0 else kc_ref[rows[sub - 1], :]
        return _block_scores(q_ref[rows[sub], :], k_prev, kc_ref[rows[sub], :], km_ref[...])

    pending = scores(0)
    for sub in range(ATTN_BLOCKS):
        ready, pending = pending, (scores(sub + 1) if sub + 1 < ATTN_BLOCKS else None)
        vt_prev = vtp_ref[...] if sub == 0 else vtc_ref[:, rows[sub - 1]]
        o_ref[rows[sub], :] = _block_finish(sink_ref, ready, vt_prev, vtc_ref[:, rows[sub]], vtm_ref[...],
                                            first_bias if sub == 0 else 0.0)


def _attention(sinks, q, k, vt, k_meta, vt_meta, *, batch, n_blocks):
    n, qd = q.shape
    kd = k.shape[1]
    steps = n_blocks // ATTN_BLOCKS
    rows = ATTN_BLOCKS * BLOCK
    cur = lambda b, j: (b * steps + j, 0)
    prev = lambda b, j: (b * n_blocks + jnp.maximum(j * ATTN_BLOCKS - 1, 0), 0)
    cur_t = lambda b, j: (0, b * steps + j)
    prev_t = lambda b, j: (0, b * n_blocks + jnp.maximum(j * ATTN_BLOCKS - 1, 0))
    return pl.pallas_call(
        _attn_kernel,
        grid=(batch, steps),
        in_specs=[
            pl.BlockSpec(memory_space=pltpu.SMEM),
            pl.BlockSpec((rows, qd), cur),
            pl.BlockSpec((BLOCK, kd), prev), pl.BlockSpec((rows, kd), cur), _const_spec((N_META, kd)),
            pl.BlockSpec((kd, BLOCK), prev_t), pl.BlockSpec((kd, rows), cur_t), _const_spec((kd, BLOCK)),
        ],
        out_specs=pl.BlockSpec((rows, qd), cur),
        out_shape=jax.ShapeDtypeStruct((n, qd), _BF16),
        compiler_params=_params(("parallel", "parallel")),
        name="swa_attention",
    )(sinks, q, k, k, k_meta, vt, vt, vt_meta)


def _dup_heads(w):
    lead = w.shape[:-1]
    w = w.reshape(lead + (N_KV_HEADS, 1, HEAD_DIM))
    return jnp.broadcast_to(w, lead + (N_KV_HEADS, 2, HEAD_DIM)).reshape(lead + (N_KV_HEADS * 2 * HEAD_DIM,))


def kernel(x, meta_tokens, norm_mix, norm_ffn, conv_w_in, conv_b_in, conv_dw, conv_ln_g, conv_ln_b,
           conv_w_out, conv_b_out, kv_norm, w_kv, k_norm, w_q, q_norm, attn_sinks, w_o,
           ffn_w_gate, ffn_w_up, ffn_w_down):
    b, seq, d = x.shape
    depth = norm_mix.shape[0]
    n_a = conv_w_in.shape[0]
    assert seq % TOKEN_TILE == 0 and TOKEN_TILE % BLOCK == 0 and n_a >= 1 and depth > n_a
    assert CONV_WIDTH - 1 <= BLOCK and N_META <= BLOCK and d % CONV_CHANNELS == 0
    assert (seq // BLOCK) & (seq // BLOCK - 1) == 0 and (seq // BLOCK) % ATTN_BLOCKS == 0

    row = lambda v: v.reshape(1, -1).astype(_F32)
    col = lambda v: v.reshape(-1, 1).astype(_F32)
    kvd = w_kv.shape[1] // 2
    kv_w = (row(kv_norm), _dup_heads(w_kv[:, :kvd]).astype(_BF16), _dup_heads(w_kv[:, kvd:]).T.astype(_BF16),
            jnp.tile(k_norm.astype(_F32), 2).reshape(1, LANES))
    q_ws = [(row(norm_mix[n_a + j]), w_q[j].astype(_BF16), jnp.tile(q_norm[j].astype(_F32), 2).reshape(1, LANES))
            for j in range(depth - n_a)]
    ffn_ws = [(row(norm_ffn[i]), ffn_w_gate[i].astype(_BF16), ffn_w_up[i].astype(_BF16),
               ffn_w_down[i].astype(_BF16)) for i in range(depth)]

    hm = jnp.pad(meta_tokens.astype(_F32), ((0, BLOCK - N_META), (0, 0)))
    h = x.reshape(b * seq, d)
    k_meta = vt_meta = k_sh = vt_sh = q = None
    for i in range(n_a):
        glu_w = (row(norm_mix[i]), conv_w_in[i].T.astype(_BF16), col(conv_b_in[i]))
        taps = jnp.pad(conv_dw[i].astype(_F32)[::-1].T, ((0, 0), (0, BLOCK - CONV_WIDTH)))
        conv_tail = (col(conv_ln_g[i]), col(conv_ln_b[i]), conv_w_out[i].T.astype(_BF16), row(conv_b_out[i]))
        at_m = _glu_t(hm, *glu_w)
        at = _glu_t(h, *glu_w)
        history = jnp.pad(at_m[:, :N_META], ((0, 0), (BLOCK - N_META, 0)))
        extra = jnp.pad(jnp.stack([history, at_m], axis=1), ((0, 0), (0, EXTRA_ROWS - 2), (0, 0)))
        ct, cextra = _dwconv_t(at, extra, taps, blocks_per_seq=seq // BLOCK)
        conv_m = (cextra[:, 1, :],) + conv_tail
        conv_r = (ct,) + conv_tail
        if i == n_a - 1:
            _, _, k_meta, vt_meta = _ffn(hm, ffn_ws[i], conv=conv_m, q_w=q_ws[0], kv_w=kv_w)
            h, q, k_sh, vt_sh = _ffn(h, ffn_ws[i], conv=conv_r, q_w=q_ws[0], kv_w=kv_w)
        else:
            hm, = _ffn(hm, ffn_ws[i], conv=conv_m)
            h, = _ffn(h, ffn_ws[i], conv=conv_r)
    k_meta = k_meta[:N_META]
    vt_meta = jnp.where(lax.broadcasted_iota(jnp.int32, vt_meta.shape, 1) < N_META, vt_meta, 0)
    for j in range(depth - n_a):
        layer = n_a + j
        a = _attention(attn_sinks[j].astype(_F32), q, k_sh, vt_sh, k_meta, vt_meta, batch=b, n_blocks=seq // BLOCK)
        oproj = (a, w_o[j].astype(_BF16))
        if layer + 1 < depth:
            h, q = _ffn(h, ffn_ws[layer], oproj=oproj, q_w=q_ws[j + 1])
        else:
            h, = _ffn(h, ffn_ws[layer], oproj=oproj)
    return h.reshape(b, seq, d)
```

```python
import functools

import jax
import jax.numpy as jnp
from jax import lax
from jax.experimental import pallas as pl
from jax.experimental.pallas import tpu as pltpu

N_META = 16
CONV_WIDTH = 31
HEAD_DIM = 64
N_KV_HEADS = 4
Q_PER_KV = 4
BLOCK = 128
NORM_EPS = 1e-6
NEG_INF = -1e30

LANES = 128
CONV_CHANNELS = 16
EXTRA_ROWS = 16
STAGE_PITCH = 24
TOKEN_TILE = 512
ATTN_BLOCKS = 16
VMEM_LIMIT = 56 * 1024 * 1024

_BF16 = jnp.bfloat16
_F32 = jnp.float32


def _const_spec(shape):
    return pl.BlockSpec(shape, lambda *_: (0,) * len(shape), pipeline_mode=pl.Buffered(1))


def _params(semantics):
    return pltpu.CompilerParams(dimension_semantics=semantics, vmem_limit_bytes=VMEM_LIMIT)


def _rms(x, g):
    ms = jnp.mean(x * x, axis=-1, keepdims=True)
    return x * lax.rsqrt(ms + NORM_EPS) * g


def _silu(x):
    return x * jax.nn.sigmoid(x)


def _dot(a, b):
    return jnp.dot(a, b, preferred_element_type=_F32)


def _dot_nt(a, b):
    return lax.dot_general(a, b, (((1,), (1,)), ((), ())), preferred_element_type=_F32)


def _glu_t_kernel(h_ref, g_ref, wint_ref, bint_ref, at_ref):
    d = h_ref.shape[1]
    u = _rms(h_ref[...], g_ref[...]).astype(_BF16)
    a2t = _dot_nt(wint_ref[...], u) + bint_ref[...]
    at_ref[...] = (a2t[:d] * jax.nn.sigmoid(a2t[d:])).astype(_BF16)


def _glu_t(h, g, w_in_t, b_in_t):
    n, d = h.shape
    tm = min(TOKEN_TILE, n)
    return pl.pallas_call(
        _glu_t_kernel,
        grid=(n // tm,),
        in_specs=[pl.BlockSpec((tm, d), lambda i: (i, 0)), _const_spec((1, d)),
                  _const_spec((2 * d, d)), _const_spec((2 * d, 1))],
        out_specs=pl.BlockSpec((d, tm), lambda i: (0, i)),
        out_shape=jax.ShapeDtypeStruct((d, n), _BF16),
        compiler_params=_params(("parallel",)),
        name="glu_t",
    )(h, g, w_in_t, b_in_t)


def _dwconv_t_kernel(at_ref, extra_ref, taps_ref, ct_ref, cextra_ref, stage_ref, *, blocks_per_seq):
    cb, nb, _ = at_ref.shape
    u_idx = lax.broadcasted_iota(jnp.int32, (BLOCK, BLOCK), 0)
    t_idx = lax.broadcasted_iota(jnp.int32, (BLOCK, BLOCK), 1)
    in_block = t_idx >= u_idx
    seq_start = (lax.broadcasted_iota(jnp.int32, (nb, BLOCK), 0) & (blocks_per_seq - 1)) == 0
    for ci in range(cb):
        circ = pltpu.roll(jnp.broadcast_to(taps_ref[ci:ci + 1, :], (BLOCK, BLOCK)), 0, 1, stride=1, stride_axis=0)
        w = jnp.concatenate([jnp.where(in_block, circ, 0.0), jnp.where(in_block, 0.0, circ)], axis=1).astype(_BF16)
        x = jnp.concatenate([at_ref[ci], extra_ref[ci]], axis=0)
        y = _dot(x, w)
        from_prev = jnp.where(seq_start, y[nb:nb + 1, BLOCK:], pltpu.roll(y[:nb, BLOCK:], 1, axis=0))
        stage_ref[pl.ds(ci, nb, stride=STAGE_PITCH), :] = y[:nb, :BLOCK] + from_prev
        cextra_ref[ci] = y[nb:, :BLOCK]
    for blk in range(nb):
        ct_ref[:, blk * BLOCK:(blk + 1) * BLOCK] = stage_ref[blk * STAGE_PITCH:blk * STAGE_PITCH + cb, :]


def _dwconv_t(at, extra, taps, *, blocks_per_seq):
    d, n = at.shape
    nb = n // BLOCK
    cb = CONV_CHANNELS
    chan = lambda i: (i, 0, 0)
    return pl.pallas_call(
        functools.partial(_dwconv_t_kernel, blocks_per_seq=blocks_per_seq),
        grid=(d // cb,),
        in_specs=[pl.BlockSpec((cb, nb, BLOCK), chan), pl.BlockSpec((cb, EXTRA_ROWS, BLOCK), chan),
                  pl.BlockSpec((cb, BLOCK), lambda i: (i, 0))],
        out_specs=[pl.BlockSpec((cb, n), lambda i: (i, 0)), pl.BlockSpec((cb, EXTRA_ROWS, BLOCK), chan)],
        out_shape=[jax.ShapeDtypeStruct((d, n), _F32), jax.ShapeDtypeStruct((d, EXTRA_ROWS, BLOCK), _F32)],
        scratch_shapes=[pltpu.VMEM((nb * STAGE_PITCH, BLOCK), _F32)],
        compiler_params=_params(("parallel",)),
        name="dwconv_t",
    )(at.reshape(d, nb, BLOCK), extra, taps)


def _head_pair_rms(x, gain_pair, scale):
    low = lax.broadcasted_iota(jnp.int32, x.shape, 1) < HEAD_DIM
    sq = x * x
    ms_lo = jnp.sum(jnp.where(low, sq, 0.0), axis=-1, keepdims=True) * (1.0 / HEAD_DIM)
    ms_hi = jnp.sum(jnp.where(low, 0.0, sq), axis=-1, keepdims=True) * (1.0 / HEAD_DIM)
    inv = jnp.where(low, lax.rsqrt(ms_lo + NORM_EPS), lax.rsqrt(ms_hi + NORM_EPS))
    return x * inv * gain_pair * scale


def _ffn_kernel(*refs, pre_conv, pre_oproj, post_q, post_kv):
    refs = list(refs)
    h_ref = refs.pop(0)
    h = h_ref[...]
    if pre_conv:
        ct_ref, lng_ref, lnb_ref, woutt_ref, bout_ref = (refs.pop(0) for _ in range(5))
        c = ct_ref[...]
        mu = jnp.mean(c, axis=0, keepdims=True)
        cen = c - mu
        var = jnp.mean(cen * cen, axis=0, keepdims=True)
        y = _silu(cen * lax.rsqrt(var + NORM_EPS) * lng_ref[...] + lnb_ref[...]).astype(_BF16)
        h = h + lax.dot_general(y, woutt_ref[...], (((0,), (1,)), ((), ())), preferred_element_type=_F32) + bout_ref[...]
    if pre_oproj:
        a_ref, wo_ref = refs.pop(0), refs.pop(0)
        h = h + _dot(a_ref[...], wo_ref[...])
    g_ref, wg_ref, wu_ref, wd_ref = (refs.pop(0) for _ in range(4))
    if post_q:
        gq_ref, wq_ref, qg_ref = (refs.pop(0) for _ in range(3))
    if post_kv:
        gkv_ref, wk_ref, wvt_ref, kg_ref = (refs.pop(0) for _ in range(4))
    o_ref = refs.pop(0)

    u = _rms(h, g_ref[...]).astype(_BF16)
    z = (_silu(_dot(u, wg_ref[...])) * _dot(u, wu_ref[...])).astype(_BF16)
    h = h + _dot(z, wd_ref[...])
    o_ref[...] = h

    if post_q:
        q_ref = refs.pop(0)
        q = _dot(_rms(h, gq_ref[...]).astype(_BF16), wq_ref[...])
        for c in range(q_ref.shape[1] // LANES):
            cl = slice(c * LANES, (c + 1) * LANES)
            q_ref[:, cl] = _head_pair_rms(q[:, cl], qg_ref[...], HEAD_DIM ** -0.5).astype(_BF16)
    if post_kv:
        k_ref, vt_ref = refs.pop(0), refs.pop(0)
        ukv = _rms(h, gkv_ref[...]).astype(_BF16)
        k = _dot(ukv, wk_ref[...])
        for c in range(k_ref.shape[1] // LANES):
            cl = slice(c * LANES, (c + 1) * LANES)
            k_ref[:, cl] = _rms(k[:, cl], kg_ref[...]).astype(_BF16)
        vt_ref[...] = _dot_nt(wvt_ref[...], ukv).astype(_BF16)


def _ffn(h, ffn_w, *, conv=None, oproj=None, q_w=None, kv_w=None):
    n, d = h.shape
    f = ffn_w[1].shape[1]
    tm = min(TOKEN_TILE, n)
    row = lambda i: (i, 0)
    args, in_specs = [h], [pl.BlockSpec((tm, d), row)]
    if conv is not None:
        args += list(conv)
        in_specs += [pl.BlockSpec((d, tm), lambda i: (0, i))] + [_const_spec(a.shape) for a in conv[1:]]
    if oproj is not None:
        args += list(oproj)
        in_specs += [pl.BlockSpec((tm, oproj[0].shape[1]), row), _const_spec(oproj[1].shape)]
    args += list(ffn_w)
    in_specs += [_const_spec((1, d)), _const_spec((d, f)), _const_spec((d, f)), _const_spec((f, d))]
    out_specs = [pl.BlockSpec((tm, d), row)]
    out_shape = [jax.ShapeDtypeStruct((n, d), _F32)]
    if q_w is not None:
        qd = q_w[1].shape[1]
        args += list(q_w)
        in_specs += [_const_spec(a.shape) for a in q_w]
        out_specs.append(pl.BlockSpec((tm, qd), row))
        out_shape.append(jax.ShapeDtypeStruct((n, qd), _BF16))
    if kv_w is not None:
        kd = kv_w[1].shape[1]
        args += list(kv_w)
        in_specs += [_const_spec(a.shape) for a in kv_w]
        out_specs += [pl.BlockSpec((tm, kd), row), pl.BlockSpec((kd, tm), lambda i: (0, i))]
        out_shape += [jax.ShapeDtypeStruct((n, kd), _BF16), jax.ShapeDtypeStruct((kd, n), _BF16)]
    return pl.pallas_call(
        functools.partial(_ffn_kernel, pre_conv=conv is not None, pre_oproj=oproj is not None,
                          post_q=q_w is not None, post_kv=kv_w is not None),
        grid=(n // tm,),
        in_specs=in_specs, out_specs=out_specs, out_shape=out_shape,
        compiler_params=_params(("parallel",)),
        name="ffn",
    )(*args)


def _block_scores(q, k_prev, k_cur, k_meta):
    low = lax.broadcasted_iota(jnp.int32, (BLOCK, LANES), 1) < HEAD_DIM
    zero_bf = jnp.zeros((BLOCK, LANES), _BF16)
    scores = []
    for g in range(N_KV_HEADS):
        gl = slice(g * LANES, (g + 1) * LANES)
        parts = []
        for p in range(Q_PER_KV // 2):
            qp = q[:, (2 * g + p) * LANES:(2 * g + p + 1) * LANES]
            parts += [jnp.where(low, qp, zero_bf), jnp.where(low, zero_bf, qp)]
        qs = jnp.concatenate(parts, axis=0)
        keys = jnp.concatenate([k_prev[:, gl], k_cur[:, gl], k_meta[:, gl]], axis=0)
        st = _dot_nt(keys, qs)
        scores.append((st[:2 * BLOCK], st[2 * BLOCK:]))
    return scores


def _block_finish(sink_ref, scores, vt_prev, vt_cur, vt_meta, prev_bias):
    nq = Q_PER_KV * BLOCK
    row_low = lax.broadcasted_iota(jnp.int32, (LANES, BLOCK), 0) < HEAD_DIM
    c = lax.broadcasted_iota(jnp.int32, (BLOCK, nq), 0)
    r = lax.broadcasted_iota(jnp.int32, (BLOCK, nq), 1) & (BLOCK - 1)
    use_cur = c <= r
    meta_pad = jnp.zeros((BLOCK - N_META, nq), _BF16)
    out = []
    for g in range(N_KV_HEADS):
        gl = slice(g * LANES, (g + 1) * LANES)
        st, sm = scores[g]
        s = jnp.where(use_cur, st[BLOCK:], st[:BLOCK] + prev_bias)
        sink = jnp.concatenate(
            [jnp.full((1, BLOCK), sink_ref[g * Q_PER_KV + hh], _F32) for hh in range(Q_PER_KV)], axis=1)
        mx = jnp.maximum(jnp.maximum(jnp.max(s, axis=0, keepdims=True),
                                     jnp.max(sm, axis=0, keepdims=True)), sink)
        pe = jnp.exp(s - mx)
        pme = jnp.exp(sm - mx)
        denom = jnp.sum(pe, axis=0, keepdims=True) + jnp.sum(pme, axis=0, keepdims=True) + jnp.exp(sink - mx)
        p_cur = jnp.where(use_cur, pe, 0.0)
        pt = jnp.concatenate([(pe - p_cur).astype(_BF16), p_cur.astype(_BF16), pme.astype(_BF16), meta_pad],
                             axis=0)
        vt = jnp.concatenate([vt_prev[gl, :], vt_cur[gl, :], vt_meta[gl, :]], axis=1)
        ot = _dot(vt, pt) / denom
        for p in range(Q_PER_KV // 2):
            pair_t = jnp.where(row_low, ot[:, 2 * p * BLOCK:(2 * p + 1) * BLOCK],
                               ot[:, (2 * p + 1) * BLOCK:(2 * p + 2) * BLOCK])
            out.append(pair_t.T.astype(_BF16))
    return jnp.concatenate(out, axis=1)


def _attn_kernel(sink_ref, q_ref, kp_ref, kc_ref, km_ref, vtp_ref, vtc_ref, vtm_ref, o_ref):
    first_bias = jnp.where(pl.program_id(1) > 0, 0.0, NEG_INF)
    rows = [slice(sub * BLOCK, (sub + 1) * BLOCK) for sub in range(ATTN_BLOCKS)]

    def scores(sub):
        k_prev = kp_ref[...] if sub == 0 else kc_ref[rows[sub - 1], :]
        return _block_scores(q_ref[rows[sub], :], k_prev, kc_ref[rows[sub], :], km_ref[...])

    pending = scores(0)
    for sub in range(ATTN_BLOCKS):
        ready, pending = pending, (scores(sub + 1) if sub + 1 < ATTN_BLOCKS else None)
        vt_prev = vtp_ref[...] if sub == 0 else vtc_ref[:, rows[sub - 1]]
        o_ref[rows[sub], :] = _block_finish(sink_ref, ready, vt_prev, vtc_ref[:, rows[sub]], vtm_ref[...],
                                            first_bias if sub == 0 else 0.0)


def _attention(sinks, q, k, vt, k_meta, vt_meta, *, batch, n_blocks):
    n, qd = q.shape
    kd = k.shape[1]
    steps = n_blocks // ATTN_BLOCKS
    rows = ATTN_BLOCKS * BLOCK
    cur = lambda b, j: (b * steps + j, 0)
    prev = lambda b, j: (b * n_blocks + jnp.maximum(j * ATTN_BLOCKS - 1, 0), 0)
    cur_t = lambda b, j: (0, b * steps + j)
    prev_t = lambda b, j: (0, b * n_blocks + jnp.maximum(j * ATTN_BLOCKS - 1, 0))
    return pl.pallas_call(
        _attn_kernel,
        grid=(batch, steps),
        in_specs=[
            pl.BlockSpec(memory_space=pltpu.SMEM),
            pl.BlockSpec((rows, qd), cur),
            pl.BlockSpec((BLOCK, kd), prev), pl.BlockSpec((rows, kd), cur), _const_spec((N_META, kd)),
            pl.BlockSpec((kd, BLOCK), prev_t), pl.BlockSpec((kd, rows), cur_t), _const_spec((kd, BLOCK)),
        ],
        out_specs=pl.BlockSpec((rows, qd), cur),
        out_shape=jax.ShapeDtypeStruct((n, qd), _BF16),
        compiler_params=_params(("parallel", "parallel")),
        name="swa_attention",
    )(sinks, q, k, k, k_meta, vt, vt, vt_meta)


def _dup_heads(w):
    lead = w.shape[:-1]
    w = w.reshape(lead + (N_KV_HEADS, 1, HEAD_DIM))
    return jnp.broadcast_to(w, lead + (N_KV_HEADS, 2, HEAD_DIM)).reshape(lead + (N_KV_HEADS * 2 * HEAD_DIM,))


def kernel(x, meta_tokens, norm_mix, norm_ffn, conv_w_in, conv_b_in, conv_dw, conv_ln_g, conv_ln_b,
           conv_w_out, conv_b_out, kv_norm, w_kv, k_norm, w_q, q_norm, attn_sinks, w_o,
           ffn_w_gate, ffn_w_up, ffn_w_down):
    b, seq, d = x.shape
    depth = norm_mix.shape[0]
    n_a = conv_w_in.shape[0]
    assert seq % TOKEN_TILE == 0 and TOKEN_TILE % BLOCK == 0 and n_a >= 1 and depth > n_a
    assert CONV_WIDTH - 1 <= BLOCK and N_META <= BLOCK and d % CONV_CHANNELS == 0
    assert (seq // BLOCK) & (seq // BLOCK - 1) == 0 and (seq // BLOCK) % ATTN_BLOCKS == 0

    row = lambda v: v.reshape(1, -1).astype(_F32)
    col = lambda v: v.reshape(-1, 1).astype(_F32)
    kvd = w_kv.shape[1] // 2
    kv_w = (row(kv_norm), _dup_heads(w_kv[:, :kvd]).astype(_BF16), _dup_heads(w_kv[:, kvd:]).T.astype(_BF16),
            jnp.tile(k_norm.astype(_F32), 2).reshape(1, LANES))
    q_ws = [(row(norm_mix[n_a + j]), w_q[j].astype(_BF16), jnp.tile(q_norm[j].astype(_F32), 2).reshape(1, LANES))
            for j in range(depth - n_a)]
    ffn_ws = [(row(norm_ffn[i]), ffn_w_gate[i].astype(_BF16), ffn_w_up[i].astype(_BF16),
               ffn_w_down[i].astype(_BF16)) for i in range(depth)]

    hm = jnp.pad(meta_tokens.astype(_F32), ((0, BLOCK - N_META), (0, 0)))
    h = x.reshape(b * seq, d)
    k_meta = vt_meta = k_sh = vt_sh = q = None
    for i in range(n_a):
        glu_w = (row(norm_mix[i]), conv_w_in[i].T.astype(_BF16), col(conv_b_in[i]))
        taps = jnp.pad(conv_dw[i].astype(_F32)[::-1].T, ((0, 0), (0, BLOCK - CONV_WIDTH)))
        conv_tail = (col(conv_ln_g[i]), col(conv_ln_b[i]), conv_w_out[i].T.astype(_BF16), row(conv_b_out[i]))
        at_m = _glu_t(hm, *glu_w)
        at = _glu_t(h, *glu_w)
        history = jnp.pad(at_m[:, :N_META], ((0, 0), (BLOCK - N_META, 0)))
        extra = jnp.pad(jnp.stack([history, at_m], axis=1), ((0, 0), (0, EXTRA_ROWS - 2), (0, 0)))
        ct, cextra = _dwconv_t(at, extra, taps, blocks_per_seq=seq // BLOCK)
        conv_m = (cextra[:, 1, :],) + conv_tail
        conv_r = (ct,) + conv_tail
        if i == n_a - 1:
            _, _, k_meta, vt_meta = _ffn(hm, ffn_ws[i], conv=conv_m, q_w=q_ws[0], kv_w=kv_w)
            h, q, k_sh, vt_sh = _ffn(h, ffn_ws[i], conv=conv_r, q_w=q_ws[0], kv_w=kv_w)
        else:
            hm, = _ffn(hm, ffn_ws[i], conv=conv_m)
            h, = _ffn(h, ffn_ws[i], conv=conv_r)
    k_meta = k_meta[:N_META]
    vt_meta = jnp.where(lax.broadcasted_iota(jnp.int32, vt_meta.shape, 1) < N_META, vt_meta, 0)
    for j in range(depth - n_a):
        layer = n_a + j
        a = _attention(attn_sinks[j].astype(_F32), q, k_sh, vt_sh, k_meta, vt_meta, batch=b, n_blocks=seq // BLOCK)
        oproj = (a, w_o[j].astype(_BF16))
        if layer + 1 < depth:
            h, q = _ffn(h, ffn_ws[layer], oproj=oproj, q_w=q_ws[j + 1])
        else:
            h, = _ffn(h, ffn_ws[layer], oproj=oproj)
    return h.reshape(b, seq, d)
```

```python
import functools
from typing import NamedTuple

import jax
import jax.numpy as jnp
from jax import lax
from jax.experimental import pallas as pl
from jax.experimental.pallas import tpu as pltpu

N_META = 16
CONV_WIDTH = 31
HEAD_DIM = 64
N_KV_HEADS = 4
Q_PER_KV = 4
BLOCK = 128
NORM_EPS = 1e-6
NEG_INF = -1e30

LANES = 128
CONV_CHANNELS = 16
EXTRA_ROWS = 16
STAGE_PITCH = 24
TOKEN_TILE = 512
ATTN_BLOCKS = 16
VMEM_LIMIT = 56 * 1024 * 1024

_BF16 = jnp.bfloat16
_F32 = jnp.float32


class _Layer(NamedTuple):
    stack: jax.Array
    index: int

    @property
    def shape(self):
        return self.stack.shape[1:]


def _const_spec(a):
    if isinstance(a, _Layer):
        zeros = (0,) * len(a.shape)
        return pl.BlockSpec((None,) + a.shape, lambda *_: (a.index,) + zeros, pipeline_mode=pl.Buffered(1))
    return pl.BlockSpec(a.shape, lambda *_: (0,) * a.ndim, pipeline_mode=pl.Buffered(1))


def _arrays(operands):
    return [a.stack if isinstance(a, _Layer) else a for a in operands]


def _params(semantics):
    return pltpu.CompilerParams(dimension_semantics=semantics, vmem_limit_bytes=VMEM_LIMIT)


def _rms(x, g):
    ms = jnp.mean(x * x, axis=-1, keepdims=True)
    return x * lax.rsqrt(ms + NORM_EPS) * g


def _silu(x):
    return x * jax.nn.sigmoid(x)


def _dot(a, b):
    return jnp.dot(a, b, preferred_element_type=_F32)


def _dot_nt(a, b):
    return lax.dot_general(a, b, (((1,), (1,)), ((), ())), preferred_element_type=_F32)


def _glu_rows_t(h, g_ref, wint_ref, bint_ref):
    d = h.shape[1]
    u = _rms(h, g_ref[...]).astype(_BF16)
    a2t = _dot_nt(wint_ref[...], u) + bint_ref[...]
    return (a2t[:d] * jax.nn.sigmoid(a2t[d:])).astype(_BF16)


def _glu_t_kernel(h_ref, g_ref, wint_ref, bint_ref, at_ref):
    at_ref[...] = _glu_rows_t(h_ref[...], g_ref, wint_ref, bint_ref)


def _glu_t(h, glu_w):
    n, d = h.shape
    tm = min(TOKEN_TILE, n)
    return pl.pallas_call(
        _glu_t_kernel,
        grid=(n // tm,),
        in_specs=[pl.BlockSpec((tm, d), lambda i: (i, 0))] + [_const_spec(a) for a in glu_w],
        out_specs=pl.BlockSpec((d, tm), lambda i: (0, i)),
        out_shape=jax.ShapeDtypeStruct((d, n), _BF16),
        compiler_params=_params(("parallel",)),
        name="glu_t",
    )(h, *_arrays(glu_w))


def _dwconv_t_kernel(at_ref, extra_ref, taps_ref, ct_ref, cextra_ref, stage_ref, *, blocks_per_seq):
    cb, nb, _ = at_ref.shape
    u_idx = lax.broadcasted_iota(jnp.int32, (BLOCK, BLOCK), 0)
    t_idx = lax.broadcasted_iota(jnp.int32, (BLOCK, BLOCK), 1)
    in_block = t_idx >= u_idx
    seq_start = (lax.broadcasted_iota(jnp.int32, (nb, BLOCK), 0) & (blocks_per_seq - 1)) == 0
    for ci in range(cb):
        circ = pltpu.roll(jnp.broadcast_to(taps_ref[ci:ci + 1, :], (BLOCK, BLOCK)), 0, 1, stride=1, stride_axis=0)
        w = jnp.concatenate([jnp.where(in_block, circ, 0.0), jnp.where(in_block, 0.0, circ)], axis=1).astype(_BF16)
        x = jnp.concatenate([at_ref[ci], extra_ref[ci]], axis=0)
        y = _dot(x, w)
        from_prev = jnp.where(seq_start, y[nb:nb + 1, BLOCK:], pltpu.roll(y[:nb, BLOCK:], 1, axis=0))
        stage_ref[pl.ds(ci, nb, stride=STAGE_PITCH), :] = y[:nb, :BLOCK] + from_prev
        cextra_ref[ci] = y[nb:, :BLOCK]
    for blk in range(nb):
        ct_ref[:, blk * BLOCK:(blk + 1) * BLOCK] = stage_ref[blk * STAGE_PITCH:blk * STAGE_PITCH + cb, :]


def _dwconv_t(at, extra, taps, *, blocks_per_seq):
    d, n = at.shape
    nb = n // BLOCK
    cb = CONV_CHANNELS
    chan = lambda i: (i, 0, 0)
    return pl.pallas_call(
        functools.partial(_dwconv_t_kernel, blocks_per_seq=blocks_per_seq),
        grid=(d // cb,),
        in_specs=[pl.BlockSpec((cb, nb, BLOCK), chan), pl.BlockSpec((cb, EXTRA_ROWS, BLOCK), chan),
                  pl.BlockSpec((cb, BLOCK), lambda i: (i, 0))],
        out_specs=[pl.BlockSpec((cb, n), lambda i: (i, 0)), pl.BlockSpec((cb, EXTRA_ROWS, BLOCK), chan)],
        out_shape=[jax.ShapeDtypeStruct((d, n), _F32), jax.ShapeDtypeStruct((d, EXTRA_ROWS, BLOCK), _F32)],
        scratch_shapes=[pltpu.VMEM((nb * STAGE_PITCH, BLOCK), _F32)],
        compiler_params=_params(("parallel",)),
        name="dwconv_t",
    )(at.reshape(d, nb, BLOCK), extra, taps)


def _head_pair_rms(x, gain_pair, scale):
    low = lax.broadcasted_iota(jnp.int32, x.shape, 1) < HEAD_DIM
    sq = x * x
    ms_lo = jnp.sum(jnp.where(low, sq, 0.0), axis=-1, keepdims=True) * (1.0 / HEAD_DIM)
    ms_hi = jnp.sum(jnp.where(low, 0.0, sq), axis=-1, keepdims=True) * (1.0 / HEAD_DIM)
    inv = jnp.where(low, lax.rsqrt(ms_lo + NORM_EPS), lax.rsqrt(ms_hi + NORM_EPS))
    return x * inv * gain_pair * scale


def _ffn_kernel(*refs, pre_conv, pre_oproj, post_q, post_kv, post_glu):
    refs = list(refs)
    h_ref = refs.pop(0)
    h = h_ref[...]
    if pre_conv:
        ct_ref, lng_ref, lnb_ref, woutt_ref, bout_ref = (refs.pop(0) for _ in range(5))
        c = ct_ref[...]
        mu = jnp.mean(c, axis=0, keepdims=True)
        cen = c - mu
        var = jnp.mean(cen * cen, axis=0, keepdims=True)
        y = _silu(cen * lax.rsqrt(var + NORM_EPS) * lng_ref[...] + lnb_ref[...]).astype(_BF16)
        h = h + lax.dot_general(y, woutt_ref[...], (((0,), (1,)), ((), ())), preferred_element_type=_F32) + bout_ref[...]
    if pre_oproj:
        a_ref, wo_ref = refs.pop(0), refs.pop(0)
        h = h + _dot(a_ref[...], wo_ref[...])
    g_ref, wg_ref, wu_ref, wd_ref = (refs.pop(0) for _ in range(4))
    if post_q:
        gq_ref, wq_ref, qg_ref = (refs.pop(0) for _ in range(3))
    if post_kv:
        gkv_ref, wk_ref, wvt_ref, kg_ref = (refs.pop(0) for _ in range(4))
    if post_glu:
        gglu_ref, wint_ref, bint_ref = (refs.pop(0) for _ in range(3))
    o_ref = refs.pop(0)

    u = _rms(h, g_ref[...]).astype(_BF16)
    z = (_silu(_dot(u, wg_ref[...])) * _dot(u, wu_ref[...])).astype(_BF16)
    h = h + _dot(z, wd_ref[...])
    o_ref[...] = h

    if post_q:
        q_ref = refs.pop(0)
        q = _dot(_rms(h, gq_ref[...]).astype(_BF16), wq_ref[...])
        for c in range(q_ref.shape[1] // LANES):
            cl = slice(c * LANES, (c + 1) * LANES)
            q_ref[:, cl] = _head_pair_rms(q[:, cl], qg_ref[...], HEAD_DIM ** -0.5).astype(_BF16)
    if post_kv:
        k_ref, vt_ref = refs.pop(0), refs.pop(0)
        ukv = _rms(h, gkv_ref[...]).astype(_BF16)
        k = _dot(ukv, wk_ref[...])
        for c in range(k_ref.shape[1] // LANES):
            cl = slice(c * LANES, (c + 1) * LANES)
            k_ref[:, cl] = _rms(k[:, cl], kg_ref[...]).astype(_BF16)
        vt_ref[...] = _dot_nt(wvt_ref[...], ukv).astype(_BF16)
    if post_glu:
        at_ref = refs.pop(0)
        at_ref[...] = _glu_rows_t(h, gglu_ref, wint_ref, bint_ref)


def _ffn(h, ffn_w, *, conv=None, oproj=None, q_w=None, kv_w=None, glu_w=None):
    n, d = h.shape
    tm = min(TOKEN_TILE, n)
    row = lambda i: (i, 0)
    col = lambda i: (0, i)
    args, in_specs = [h], [pl.BlockSpec((tm, d), row)]
    if conv is not None:
        args += list(conv)
        in_specs += [pl.BlockSpec((d, tm), col)] + [_const_spec(a) for a in conv[1:]]
    if oproj is not None:
        args += list(oproj)
        in_specs += [pl.BlockSpec((tm, oproj[0].shape[1]), row), _const_spec(oproj[1])]
    args += list(ffn_w)
    in_specs += [_const_spec(a) for a in ffn_w]
    out_specs = [pl.BlockSpec((tm, d), row)]
    out_shape = [jax.ShapeDtypeStruct((n, d), _F32)]
    if q_w is not None:
        qd = q_w[1].shape[1]
        args += list(q_w)
        in_specs += [_const_spec(a) for a in q_w]
        out_specs.append(pl.BlockSpec((tm, qd), row))
        out_shape.append(jax.ShapeDtypeStruct((n, qd), _BF16))
    if kv_w is not None:
        kd = kv_w[1].shape[1]
        args += list(kv_w)
        in_specs += [_const_spec(a) for a in kv_w]
        out_specs += [pl.BlockSpec((tm, kd), row), pl.BlockSpec((kd, tm), col)]
        out_shape += [jax.ShapeDtypeStruct((n, kd), _BF16), jax.ShapeDtypeStruct((kd, n), _BF16)]
    if glu_w is not None:
        args += list(glu_w)
        in_specs += [_const_spec(a) for a in glu_w]
        out_specs.append(pl.BlockSpec((d, tm), col))
        out_shape.append(jax.ShapeDtypeStruct((d, n), _BF16))
    return pl.pallas_call(
        functools.partial(_ffn_kernel, pre_conv=conv is not None, pre_oproj=oproj is not None,
                          post_q=q_w is not None, post_kv=kv_w is not None, post_glu=glu_w is not None),
        grid=(n // tm,),
        in_specs=in_specs, out_specs=out_specs, out_shape=out_shape,
        compiler_params=_params(("parallel",)),
        name="ffn",
    )(*_arrays(args))


def _block_scores(q, k_prev, k_cur, k_meta):
    low = lax.broadcasted_iota(jnp.int32, (BLOCK, LANES), 1) < HEAD_DIM
    zero_bf = jnp.zeros((BLOCK, LANES), _BF16)
    scores = []
    for g in range(N_KV_HEADS):
        gl = slice(g * LANES, (g + 1) * LANES)
        parts = []
        for p in range(Q_PER_KV // 2):
            qp = q[:, (2 * g + p) * LANES:(2 * g + p + 1) * LANES]
            parts += [jnp.where(low, qp, zero_bf), jnp.where(low, zero_bf, qp)]
        qs = jnp.concatenate(parts, axis=0)
        keys = jnp.concatenate([k_prev[:, gl], k_cur[:, gl], k_meta[:, gl]], axis=0)
        st = _dot_nt(keys, qs)
        scores.append((st[:2 * BLOCK], st[2 * BLOCK:]))
    return scores


def _block_finish(sink_ref, scores, vt_prev, vt_cur, vt_meta, prev_bias):
    nq = Q_PER_KV * BLOCK
    row_low = lax.broadcasted_iota(jnp.int32, (LANES, BLOCK), 0) < HEAD_DIM
    c = lax.broadcasted_iota(jnp.int32, (BLOCK, nq), 0)
    r = lax.broadcasted_iota(jnp.int32, (BLOCK, nq), 1) & (BLOCK - 1)
    use_cur = c <= r
    meta_pad = jnp.zeros((BLOCK - N_META, nq), _BF16)
    out = []
    for g in range(N_KV_HEADS):
        gl = slice(g * LANES, (g + 1) * LANES)
        st, sm = scores[g]
        s = jnp.where(use_cur, st[BLOCK:], st[:BLOCK] + prev_bias)
        sink = jnp.concatenate(
            [jnp.full((1, BLOCK), sink_ref[g * Q_PER_KV + hh], _F32) for hh in range(Q_PER_KV)], axis=1)
        mx = jnp.maximum(jnp.maximum(jnp.max(s, axis=0, keepdims=True),
                                     jnp.max(sm, axis=0, keepdims=True)), sink)
        pe = jnp.exp(s - mx)
        pme = jnp.exp(sm - mx)
        denom = jnp.sum(pe, axis=0, keepdims=True) + jnp.sum(pme, axis=0, keepdims=True) + jnp.exp(sink - mx)
        p_cur = jnp.where(use_cur, pe, 0.0)
        pt = jnp.concatenate([(pe - p_cur).astype(_BF16), p_cur.astype(_BF16), pme.astype(_BF16), meta_pad],
                             axis=0)
        vt = jnp.concatenate([vt_prev[gl, :], vt_cur[gl, :], vt_meta[gl, :]], axis=1)
        ot = _dot(vt, pt) / denom
        for p in range(Q_PER_KV // 2):
            pair_t = jnp.where(row_low, ot[:, 2 * p * BLOCK:(2 * p + 1) * BLOCK],
                               ot[:, (2 * p + 1) * BLOCK:(2 * p + 2) * BLOCK])
            out.append(pair_t.T.astype(_BF16))
    return jnp.concatenate(out, axis=1)


def _attn_kernel(sink_ref, q_ref, kp_ref, kc_ref, km_ref, vtp_ref, vtc_ref, vtm_ref, o_ref):
    first_bias = jnp.where(pl.program_id(1) > 0, 0.0, NEG_INF)
    rows = [slice(sub * BLOCK, (sub + 1) * BLOCK) for sub in range(ATTN_BLOCKS)]

    def scores(sub):
        k_prev = kp_ref[...] if sub == 0 else kc_ref[rows[sub - 1], :]
        return _block_scores(q_ref[rows[sub], :], k_prev, kc_ref[rows[sub], :], km_ref[...])

    pending = scores(0)
    for sub in range(ATTN_BLOCKS):
        ready, pending = pending, (scores(sub + 1) if sub + 1 < ATTN_BLOCKS else None)
        vt_prev = vtp_ref[...] if sub == 0 else vtc_ref[:, rows[sub - 1]]
        o_ref[rows[sub], :] = _block_finish(sink_ref, ready, vt_prev, vtc_ref[:, rows[sub]], vtm_ref[...],
                                            first_bias if sub == 0 else 0.0)


def _attention(sinks, q, k, vt, k_meta, vt_meta, *, batch, n_blocks):
    n, qd = q.shape
    kd = k.shape[1]
    steps = n_blocks // ATTN_BLOCKS
    rows = ATTN_BLOCKS * BLOCK
    cur = lambda b, j: (b * steps + j, 0)
    prev = lambda b, j: (b * n_blocks + jnp.maximum(j * ATTN_BLOCKS - 1, 0), 0)
    cur_t = lambda b, j: (0, b * steps + j)
    prev_t = lambda b, j: (0, b * n_blocks + jnp.maximum(j * ATTN_BLOCKS - 1, 0))
    return pl.pallas_call(
        _attn_kernel,
        grid=(batch, steps),
        in_specs=[
            pl.BlockSpec(memory_space=pltpu.SMEM),
            pl.BlockSpec((rows, qd), cur),
            pl.BlockSpec((BLOCK, kd), prev), pl.BlockSpec((rows, kd), cur), _const_spec(k_meta),
            pl.BlockSpec((kd, BLOCK), prev_t), pl.BlockSpec((kd, rows), cur_t), _const_spec(vt_meta),
        ],
        out_specs=pl.BlockSpec((rows, qd), cur),
        out_shape=jax.ShapeDtypeStruct((n, qd), _BF16),
        compiler_params=_params(("parallel", "parallel")),
        name="swa_attention",
    )(sinks, q, k, k, k_meta, vt, vt, vt_meta)


def _dup_heads(w):
    lead = w.shape[:-1]
    w = w.reshape(lead + (N_KV_HEADS, 1, HEAD_DIM))
    return jnp.broadcast_to(w, lead + (N_KV_HEADS, 2, HEAD_DIM)).reshape(lead + (N_KV_HEADS * 2 * HEAD_DIM,))


def kernel(x, meta_tokens, norm_mix, norm_ffn, conv_w_in, conv_b_in, conv_dw, conv_ln_g, conv_ln_b,
           conv_w_out, conv_b_out, kv_norm, w_kv, k_norm, w_q, q_norm, attn_sinks, w_o,
           ffn_w_gate, ffn_w_up, ffn_w_down):
    b, seq, d = x.shape
    depth = norm_mix.shape[0]
    n_a = conv_w_in.shape[0]
    assert seq % TOKEN_TILE == 0 and TOKEN_TILE % BLOCK == 0 and n_a >= 1 and depth > n_a
    assert CONV_WIDTH - 1 <= BLOCK and N_META <= BLOCK and d % CONV_CHANNELS == 0
    assert (seq // BLOCK) & (seq // BLOCK - 1) == 0 and (seq // BLOCK) % ATTN_BLOCKS == 0

    row = lambda v: v.reshape(1, -1).astype(_F32)
    col = lambda v: v.reshape(-1, 1).astype(_F32)
    kvd = w_kv.shape[1] // 2
    kv_w = (row(kv_norm), _dup_heads(w_kv[:, :kvd]).astype(_BF16), _dup_heads(w_kv[:, kvd:]).T.astype(_BF16),
            jnp.tile(k_norm.astype(_F32), 2).reshape(1, LANES))
    wq_s, wo_s = w_q.astype(_BF16), w_o.astype(_BF16)
    wg_s, wu_s, wd_s = ffn_w_gate.astype(_BF16), ffn_w_up.astype(_BF16), ffn_w_down.astype(_BF16)
    wint_s = jnp.swapaxes(conv_w_in, 1, 2).astype(_BF16)
    woutt_s = jnp.swapaxes(conv_w_out, 1, 2).astype(_BF16)
    q_ws = [(row(norm_mix[n_a + j]), _Layer(wq_s, j), jnp.tile(q_norm[j].astype(_F32), 2).reshape(1, LANES))
            for j in range(depth - n_a)]
    ffn_ws = [(row(norm_ffn[i]), _Layer(wg_s, i), _Layer(wu_s, i), _Layer(wd_s, i)) for i in range(depth)]
    glu_ws = [(row(norm_mix[i]), _Layer(wint_s, i), col(conv_b_in[i])) for i in range(n_a)]

    hm = jnp.pad(meta_tokens.astype(_F32), ((0, BLOCK - N_META), (0, 0)))
    h = x.reshape(b * seq, d)
    at_m, at = _glu_t(hm, glu_ws[0]), _glu_t(h, glu_ws[0])
    k_meta = vt_meta = k_sh = vt_sh = q = None
    for i in range(n_a):
        taps = jnp.pad(conv_dw[i].astype(_F32)[::-1].T, ((0, 0), (0, BLOCK - CONV_WIDTH)))
        conv_tail = (col(conv_ln_g[i]), col(conv_ln_b[i]), _Layer(woutt_s, i), row(conv_b_out[i]))
        history = jnp.pad(at_m[:, :N_META], ((0, 0), (BLOCK - N_META, 0)))
        extra = jnp.pad(jnp.stack([history, at_m], axis=1), ((0, 0), (0, EXTRA_ROWS - 2), (0, 0)))
        ct, cextra = _dwconv_t(at, extra, taps, blocks_per_seq=seq // BLOCK)
        conv_m = (cextra[:, 1, :],) + conv_tail
        conv_r = (ct,) + conv_tail
        if i == n_a - 1:
            _, _, k_meta, vt_meta = _ffn(hm, ffn_ws[i], conv=conv_m, q_w=q_ws[0], kv_w=kv_w)
            h, q, k_sh, vt_sh = _ffn(h, ffn_ws[i], conv=conv_r, q_w=q_ws[0], kv_w=kv_w)
        else:
            hm, at_m = _ffn(hm, ffn_ws[i], conv=conv_m, glu_w=glu_ws[i + 1])
            h, at = _ffn(h, ffn_ws[i], conv=conv_r, glu_w=glu_ws[i + 1])
    k_meta = k_meta[:N_META]
    vt_meta = jnp.where(lax.broadcasted_iota(jnp.int32, vt_meta.shape, 1) < N_META, vt_meta, 0)
    for j in range(depth - n_a):
        layer = n_a + j
        a = _attention(attn_sinks[j].astype(_F32), q, k_sh, vt_sh, k_meta, vt_meta, batch=b, n_blocks=seq // BLOCK)
        oproj = (a, _Layer(wo_s, j))
        if layer + 1 < depth:
            h, q = _ffn(h, ffn_ws[layer], oproj=oproj, q_w=q_ws[j + 1])
        else:
            h, = _ffn(h, ffn_ws[layer], oproj=oproj)
    return h.reshape(b, seq, d)
```

```python
import functools
from typing import NamedTuple

import jax
import jax.numpy as jnp
from jax import lax
from jax.experimental import pallas as pl
from jax.experimental.pallas import tpu as pltpu

N_META = 16
CONV_WIDTH = 31
HEAD_DIM = 64
N_KV_HEADS = 4
Q_PER_KV = 4
BLOCK = 128
NORM_EPS = 1e-6
NEG_INF = -1e30

LANES = 128
CONV_CHANNELS = 64
EXTRA_ROWS = 16
STAGE_PITCH = 72
TOKEN_TILE = 512
ATTN_BLOCKS = 16
VMEM_LIMIT = 56 * 1024 * 1024

_BF16 = jnp.bfloat16
_F32 = jnp.float32


class _Layer(NamedTuple):
    stack: jax.Array
    index: int

    @property
    def shape(self):
        return self.stack.shape[1:]


def _const_spec(a):
    if isinstance(a, _Layer):
        zeros = (0,) * len(a.shape)
        return pl.BlockSpec((None,) + a.shape, lambda *_: (a.index,) + zeros, pipeline_mode=pl.Buffered(1))
    return pl.BlockSpec(a.shape, lambda *_: (0,) * a.ndim, pipeline_mode=pl.Buffered(1))


def _arrays(operands):
    return [a.stack if isinstance(a, _Layer) else a for a in operands]


def _params(semantics):
    return pltpu.CompilerParams(dimension_semantics=semantics, vmem_limit_bytes=VMEM_LIMIT)


def _rms(x, g):
    ms = jnp.mean(x * x, axis=-1, keepdims=True)
    return x * lax.rsqrt(ms + NORM_EPS) * g


def _silu(x):
    return x * jax.nn.sigmoid(x)


def _dot(a, b):
    return jnp.dot(a, b, preferred_element_type=_F32)


def _dot_nt(a, b):
    return lax.dot_general(a, b, (((1,), (1,)), ((), ())), preferred_element_type=_F32)


def _glu_rows_t(h, g_ref, wint_ref, bint_ref):
    d = h.shape[1]
    u = _rms(h, g_ref[...]).astype(_BF16)
    a2t = _dot_nt(wint_ref[...], u) + bint_ref[...]
    return (a2t[:d] * jax.nn.sigmoid(a2t[d:])).astype(_BF16)


def _glu_t_kernel(h_ref, g_ref, wint_ref, bint_ref, at_ref):
    at_ref[...] = _glu_rows_t(h_ref[...], g_ref, wint_ref, bint_ref)


def _glu_t(h, glu_w):
    n, d = h.shape
    tm = min(2 * TOKEN_TILE, n)
    return pl.pallas_call(
        _glu_t_kernel,
        grid=(n // tm,),
        in_specs=[pl.BlockSpec((tm, d), lambda i: (i, 0))] + [_const_spec(a) for a in glu_w],
        out_specs=pl.BlockSpec((d, tm), lambda i: (0, i)),
        out_shape=jax.ShapeDtypeStruct((d, n), _BF16),
        compiler_params=_params(("parallel",)),
        name="glu_t",
    )(h, *_arrays(glu_w))


def _dwconv_t_kernel(at_ref, extra_ref, taps_ref, ct_ref, cextra_ref, stage_ref, *, blocks_per_seq):
    cb, nb, _ = at_ref.shape
    u_idx = lax.broadcasted_iota(jnp.int32, (BLOCK, BLOCK), 0)
    t_idx = lax.broadcasted_iota(jnp.int32, (BLOCK, BLOCK), 1)
    in_block = t_idx >= u_idx
    seq_start = (lax.broadcasted_iota(jnp.int32, (nb, BLOCK), 0) & (blocks_per_seq - 1)) == 0
    for ci in range(cb):
        circ = pltpu.roll(jnp.broadcast_to(taps_ref[ci:ci + 1, :], (BLOCK, BLOCK)), 0, 1, stride=1, stride_axis=0)
        w = jnp.concatenate([jnp.where(in_block, circ, 0.0), jnp.where(in_block, 0.0, circ)], axis=1).astype(_BF16)
        x = jnp.concatenate([at_ref[ci], extra_ref[ci]], axis=0)
        y = _dot(x, w)
        from_prev = jnp.where(seq_start, y[nb:nb + 1, BLOCK:], pltpu.roll(y[:nb, BLOCK:], 1, axis=0))
        stage_ref[pl.ds(ci, nb, stride=STAGE_PITCH), :] = y[:nb, :BLOCK] + from_prev
        cextra_ref[ci] = y[nb:, :BLOCK]
    for blk in range(nb):
        ct_ref[:, blk * BLOCK:(blk + 1) * BLOCK] = stage_ref[blk * STAGE_PITCH:blk * STAGE_PITCH + cb, :]


def _dwconv_t(at, extra, taps, *, blocks_per_seq):
    d, n = at.shape
    nb = n // BLOCK
    cb = CONV_CHANNELS
    chan = lambda i: (i, 0, 0)
    return pl.pallas_call(
        functools.partial(_dwconv_t_kernel, blocks_per_seq=blocks_per_seq),
        grid=(d // cb,),
        in_specs=[pl.BlockSpec((cb, nb, BLOCK), chan), pl.BlockSpec((cb, EXTRA_ROWS, BLOCK), chan),
                  pl.BlockSpec((cb, BLOCK), lambda i: (i, 0))],
        out_specs=[pl.BlockSpec((cb, n), lambda i: (i, 0)), pl.BlockSpec((cb, EXTRA_ROWS, BLOCK), chan)],
        out_shape=[jax.ShapeDtypeStruct((d, n), _F32), jax.ShapeDtypeStruct((d, EXTRA_ROWS, BLOCK), _F32)],
        scratch_shapes=[pltpu.VMEM((nb * STAGE_PITCH, BLOCK), _F32)],
        compiler_params=_params(("parallel",)),
        name="dwconv_t",
    )(at.reshape(d, nb, BLOCK), extra, taps)


def _head_pair_rms(x, gain_pair, scale):
    low = lax.broadcasted_iota(jnp.int32, x.shape, 1) < HEAD_DIM
    sq = x * x
    ms_lo = jnp.sum(jnp.where(low, sq, 0.0), axis=-1, keepdims=True) * (1.0 / HEAD_DIM)
    ms_hi = jnp.sum(jnp.where(low, 0.0, sq), axis=-1, keepdims=True) * (1.0 / HEAD_DIM)
    inv = jnp.where(low, lax.rsqrt(ms_lo + NORM_EPS), lax.rsqrt(ms_hi + NORM_EPS))
    return x * inv * gain_pair * scale


def _ffn_kernel(*refs, pre_conv, pre_oproj, post_q, post_kv, post_glu):
    refs = list(refs)
    h_ref = refs.pop(0)
    h = h_ref[...]
    if pre_conv:
        ct_ref, lng_ref, lnb_ref, woutt_ref, bout_ref = (refs.pop(0) for _ in range(5))
        c = ct_ref[...]
        mu = jnp.mean(c, axis=0, keepdims=True)
        cen = c - mu
        var = jnp.mean(cen * cen, axis=0, keepdims=True)
        y = _silu(cen * lax.rsqrt(var + NORM_EPS) * lng_ref[...] + lnb_ref[...]).astype(_BF16)
        h = h + lax.dot_general(y, woutt_ref[...], (((0,), (1,)), ((), ())), preferred_element_type=_F32) + bout_ref[...]
    if pre_oproj:
        a_ref, wo_ref = refs.pop(0), refs.pop(0)
        h = h + _dot(a_ref[...], wo_ref[...])
    g_ref, wg_ref, wu_ref, wd_ref = (refs.pop(0) for _ in range(4))
    if post_q:
        gq_ref, wq_ref, qg_ref = (refs.pop(0) for _ in range(3))
    if post_kv:
        gkv_ref, wk_ref, wvt_ref, kg_ref = (refs.pop(0) for _ in range(4))
    if post_glu:
        gglu_ref, wint_ref, bint_ref = (refs.pop(0) for _ in range(3))
    o_ref = refs.pop(0)

    u = _rms(h, g_ref[...]).astype(_BF16)
    z = (_silu(_dot(u, wg_ref[...])) * _dot(u, wu_ref[...])).astype(_BF16)
    h = h + _dot(z, wd_ref[...])
    o_ref[...] = h

    if post_q:
        q_ref = refs.pop(0)
        q = _dot(_rms(h, gq_ref[...]).astype(_BF16), wq_ref[...])
        for c in range(q_ref.shape[1] // LANES):
            cl = slice(c * LANES, (c + 1) * LANES)
            q_ref[:, cl] = _head_pair_rms(q[:, cl], qg_ref[...], HEAD_DIM ** -0.5).astype(_BF16)
    if post_kv:
        k_ref, vt_ref = refs.pop(0), refs.pop(0)
        ukv = _rms(h, gkv_ref[...]).astype(_BF16)
        k = _dot(ukv, wk_ref[...])
        for c in range(k_ref.shape[1] // LANES):
            cl = slice(c * LANES, (c + 1) * LANES)
            k_ref[:, cl] = _rms(k[:, cl], kg_ref[...]).astype(_BF16)
        vt_ref[...] = _dot_nt(wvt_ref[...], ukv).astype(_BF16)
    if post_glu:
        at_ref = refs.pop(0)
        at_ref[...] = _glu_rows_t(h, gglu_ref, wint_ref, bint_ref)


def _ffn(h, ffn_w, *, conv=None, oproj=None, q_w=None, kv_w=None, glu_w=None):
    n, d = h.shape
    tm = min(TOKEN_TILE, n)
    row = lambda i: (i, 0)
    col = lambda i: (0, i)
    args, in_specs = [h], [pl.BlockSpec((tm, d), row)]
    if conv is not None:
        args += list(conv)
        in_specs += [pl.BlockSpec((d, tm), col)] + [_const_spec(a) for a in conv[1:]]
    if oproj is not None:
        args += list(oproj)
        in_specs += [pl.BlockSpec((tm, oproj[0].shape[1]), row), _const_spec(oproj[1])]
    args += list(ffn_w)
    in_specs += [_const_spec(a) for a in ffn_w]
    out_specs = [pl.BlockSpec((tm, d), row)]
    out_shape = [jax.ShapeDtypeStruct((n, d), _F32)]
    if q_w is not None:
        qd = q_w[1].shape[1]
        args += list(q_w)
        in_specs += [_const_spec(a) for a in q_w]
        out_specs.append(pl.BlockSpec((tm, qd), row))
        out_shape.append(jax.ShapeDtypeStruct((n, qd), _BF16))
    if kv_w is not None:
        kd = kv_w[1].shape[1]
        args += list(kv_w)
        in_specs += [_const_spec(a) for a in kv_w]
        out_specs += [pl.BlockSpec((tm, kd), row), pl.BlockSpec((kd, tm), col)]
        out_shape += [jax.ShapeDtypeStruct((n, kd), _BF16), jax.ShapeDtypeStruct((kd, n), _BF16)]
    if glu_w is not None:
        args += list(glu_w)
        in_specs += [_const_spec(a) for a in glu_w]
        out_specs.append(pl.BlockSpec((d, tm), col))
        out_shape.append(jax.ShapeDtypeStruct((d, n), _BF16))
    return pl.pallas_call(
        functools.partial(_ffn_kernel, pre_conv=conv is not None, pre_oproj=oproj is not None,
                          post_q=q_w is not None, post_kv=kv_w is not None, post_glu=glu_w is not None),
        grid=(n // tm,),
        in_specs=in_specs, out_specs=out_specs, out_shape=out_shape,
        compiler_params=_params(("parallel",)),
        name="ffn",
    )(*_arrays(args))


def _block_scores(q, k_prev, k_cur, k_meta):
    low = lax.broadcasted_iota(jnp.int32, (BLOCK, LANES), 1) < HEAD_DIM
    zero_bf = jnp.zeros((BLOCK, LANES), _BF16)
    scores = []
    for g in range(N_KV_HEADS):
        gl = slice(g * LANES, (g + 1) * LANES)
        parts = []
        for p in range(Q_PER_KV // 2):
            qp = q[:, (2 * g + p) * LANES:(2 * g + p + 1) * LANES]
            parts += [jnp.where(low, qp, zero_bf), jnp.where(low, zero_bf, qp)]
        qs = jnp.concatenate(parts, axis=0)
        keys = jnp.concatenate([k_prev[:, gl], k_cur[:, gl], k_meta[:, gl]], axis=0)
        st = _dot_nt(keys, qs)
        scores.append((st[:2 * BLOCK], st[2 * BLOCK:]))
    return scores


def _block_finish(sink_ref, scores, vt_prev, vt_cur, vt_meta, prev_bias):
    nq = Q_PER_KV * BLOCK
    row_low = lax.broadcasted_iota(jnp.int32, (LANES, BLOCK), 0) < HEAD_DIM
    c = lax.broadcasted_iota(jnp.int32, (BLOCK, nq), 0)
    r = lax.broadcasted_iota(jnp.int32, (BLOCK, nq), 1) & (BLOCK - 1)
    use_cur = c <= r
    meta_pad = jnp.zeros((BLOCK - N_META, nq), _BF16)
    out = []
    for g in range(N_KV_HEADS):
        gl = slice(g * LANES, (g + 1) * LANES)
        st, sm = scores[g]
        s = jnp.where(use_cur, st[BLOCK:], st[:BLOCK] + prev_bias)
        sink = jnp.concatenate(
            [jnp.full((1, BLOCK), sink_ref[g * Q_PER_KV + hh], _F32) for hh in range(Q_PER_KV)], axis=1)
        mx = jnp.maximum(jnp.maximum(jnp.max(s, axis=0, keepdims=True),
                                     jnp.max(sm, axis=0, keepdims=True)), sink)
        pe = jnp.exp(s - mx)
        pme = jnp.exp(sm - mx)
        denom = jnp.sum(pe, axis=0, keepdims=True) + jnp.sum(pme, axis=0, keepdims=True) + jnp.exp(sink - mx)
        p_cur = jnp.where(use_cur, pe, 0.0)
        pt = jnp.concatenate([(pe - p_cur).astype(_BF16), p_cur.astype(_BF16), pme.astype(_BF16), meta_pad],
                             axis=0)
        vt = jnp.concatenate([vt_prev[gl, :], vt_cur[gl, :], vt_meta[gl, :]], axis=1)
        ot = _dot(vt, pt) / denom
        for p in range(Q_PER_KV // 2):
            pair_t = jnp.where(row_low, ot[:, 2 * p * BLOCK:(2 * p + 1) * BLOCK],
                               ot[:, (2 * p + 1) * BLOCK:(2 * p + 2) * BLOCK])
            out.append(pair_t.T.astype(_BF16))
    return jnp.concatenate(out, axis=1)


def _attn_kernel(sink_ref, q_ref, kp_ref, kc_ref, km_ref, vtp_ref, vtc_ref, vtm_ref, o_ref):
    first_bias = jnp.where(pl.program_id(1) > 0, 0.0, NEG_INF)
    rows = [slice(sub * BLOCK, (sub + 1) * BLOCK) for sub in range(ATTN_BLOCKS)]

    def scores(sub):
        k_prev = kp_ref[...] if sub == 0 else kc_ref[rows[sub - 1], :]
        return _block_scores(q_ref[rows[sub], :], k_prev, kc_ref[rows[sub], :], km_ref[...])

    pending = scores(0)
    for sub in range(ATTN_BLOCKS):
        ready, pending = pending, (scores(sub + 1) if sub + 1 < ATTN_BLOCKS else None)
        vt_prev = vtp_ref[...] if sub == 0 else vtc_ref[:, rows[sub - 1]]
        o_ref[rows[sub], :] = _block_finish(sink_ref, ready, vt_prev, vtc_ref[:, rows[sub]], vtm_ref[...],
                                            first_bias if sub == 0 else 0.0)


def _attention(sinks, q, k, vt, k_meta, vt_meta, *, batch, n_blocks):
    n, qd = q.shape
    kd = k.shape[1]
    steps = n_blocks // ATTN_BLOCKS
    rows = ATTN_BLOCKS * BLOCK
    cur = lambda b, j: (b * steps + j, 0)
    prev = lambda b, j: (b * n_blocks + jnp.maximum(j * ATTN_BLOCKS - 1, 0), 0)
    cur_t = lambda b, j: (0, b * steps + j)
    prev_t = lambda b, j: (0, b * n_blocks + jnp.maximum(j * ATTN_BLOCKS - 1, 0))
    return pl.pallas_call(
        _attn_kernel,
        grid=(batch, steps),
        in_specs=[
            pl.BlockSpec(memory_space=pltpu.SMEM),
            pl.BlockSpec((rows, qd), cur),
            pl.BlockSpec((BLOCK, kd), prev), pl.BlockSpec((rows, kd), cur), _const_spec(k_meta),
            pl.BlockSpec((kd, BLOCK), prev_t), pl.BlockSpec((kd, rows), cur_t), _const_spec(vt_meta),
        ],
        out_specs=pl.BlockSpec((rows, qd), cur),
        out_shape=jax.ShapeDtypeStruct((n, qd), _BF16),
        compiler_params=_params(("parallel", "parallel")),
        name="swa_attention",
    )(sinks, q, k, k, k_meta, vt, vt, vt_meta)


def _dup_heads(w):
    lead = w.shape[:-1]
    w = w.reshape(lead + (N_KV_HEADS, 1, HEAD_DIM))
    return jnp.broadcast_to(w, lead + (N_KV_HEADS, 2, HEAD_DIM)).reshape(lead + (N_KV_HEADS * 2 * HEAD_DIM,))


def kernel(x, meta_tokens, norm_mix, norm_ffn, conv_w_in, conv_b_in, conv_dw, conv_ln_g, conv_ln_b,
           conv_w_out, conv_b_out, kv_norm, w_kv, k_norm, w_q, q_norm, attn_sinks, w_o,
           ffn_w_gate, ffn_w_up, ffn_w_down):
    b, seq, d = x.shape
    depth = norm_mix.shape[0]
    n_a = conv_w_in.shape[0]
    assert seq % TOKEN_TILE == 0 and TOKEN_TILE % BLOCK == 0 and n_a >= 1 and depth > n_a
    assert CONV_WIDTH - 1 <= BLOCK and N_META <= BLOCK and d % CONV_CHANNELS == 0
    assert (seq // BLOCK) & (seq // BLOCK - 1) == 0 and (seq // BLOCK) % ATTN_BLOCKS == 0

    row = lambda v: v.reshape(1, -1).astype(_F32)
    col = lambda v: v.reshape(-1, 1).astype(_F32)
    kvd = w_kv.shape[1] // 2
    kv_w = (row(kv_norm), _dup_heads(w_kv[:, :kvd]).astype(_BF16), _dup_heads(w_kv[:, kvd:]).T.astype(_BF16),
            jnp.tile(k_norm.astype(_F32), 2).reshape(1, LANES))
    wq_s, wo_s = w_q.astype(_BF16), w_o.astype(_BF16)
    wg_s, wu_s, wd_s = ffn_w_gate.astype(_BF16), ffn_w_up.astype(_BF16), ffn_w_down.astype(_BF16)
    wint_s = jnp.swapaxes(conv_w_in, 1, 2).astype(_BF16)
    woutt_s = jnp.swapaxes(conv_w_out, 1, 2).astype(_BF16)
    q_ws = [(row(norm_mix[n_a + j]), _Layer(wq_s, j), jnp.tile(q_norm[j].astype(_F32), 2).reshape(1, LANES))
            for j in range(depth - n_a)]
    ffn_ws = [(row(norm_ffn[i]), _Layer(wg_s, i), _Layer(wu_s, i), _Layer(wd_s, i)) for i in range(depth)]
    glu_ws = [(row(norm_mix[i]), _Layer(wint_s, i), col(conv_b_in[i])) for i in range(n_a)]

    hm = jnp.pad(meta_tokens.astype(_F32), ((0, BLOCK - N_META), (0, 0)))
    h = x.reshape(b * seq, d)
    at_m, at = _glu_t(hm, glu_ws[0]), _glu_t(h, glu_ws[0])
    k_meta = vt_meta = k_sh = vt_sh = q = None
    for i in range(n_a):
        taps = jnp.pad(conv_dw[i].astype(_F32)[::-1].T, ((0, 0), (0, BLOCK - CONV_WIDTH)))
        conv_tail = (col(conv_ln_g[i]), col(conv_ln_b[i]), _Layer(woutt_s, i), row(conv_b_out[i]))
        history = jnp.pad(at_m[:, :N_META], ((0, 0), (BLOCK - N_META, 0)))
        extra = jnp.pad(jnp.stack([history, at_m], axis=1), ((0, 0), (0, EXTRA_ROWS - 2), (0, 0)))
        ct, cextra = _dwconv_t(at, extra, taps, blocks_per_seq=seq // BLOCK)
        conv_m = (cextra[:, 1, :],) + conv_tail
        conv_r = (ct,) + conv_tail
        if i == n_a - 1:
            _, _, k_meta, vt_meta = _ffn(hm, ffn_ws[i], conv=conv_m, q_w=q_ws[0], kv_w=kv_w)
            h, q, k_sh, vt_sh = _ffn(h, ffn_ws[i], conv=conv_r, q_w=q_ws[0], kv_w=kv_w)
        else:
            hm, at_m = _ffn(hm, ffn_ws[i], conv=conv_m, glu_w=glu_ws[i + 1])
            h, at = _ffn(h, ffn_ws[i], conv=conv_r, glu_w=glu_ws[i + 1])
    k_meta = k_meta[:N_META]
    vt_meta = jnp.where(lax.broadcasted_iota(jnp.int32, vt_meta.shape, 1) < N_META, vt_meta, 0)
    for j in range(depth - n_a):
        layer = n_a + j
        a = _attention(attn_sinks[j].astype(_F32), q, k_sh, vt_sh, k_meta, vt_meta, batch=b, n_blocks=seq // BLOCK)
        oproj = (a, _Layer(wo_s, j))
        if layer + 1 < depth:
            h, q = _ffn(h, ffn_ws[layer], oproj=oproj, q_w=q_ws[j + 1])
        else:
            h, = _ffn(h, ffn_ws[layer], oproj=oproj)
    return h.reshape(b, seq, d)
```

```python
import functools
from typing import NamedTuple

import jax
import jax.numpy as jnp
from jax import lax
from jax.experimental import pallas as pl
from jax.experimental.pallas import tpu as pltpu

N_META = 16
CONV_WIDTH = 31
HEAD_DIM = 64
N_KV_HEADS = 4
Q_PER_KV = 4
BLOCK = 128
NORM_EPS = 1e-6
NEG_INF = -1e30

LANES = 128
CONV_CHANNELS = 64
EXTRA_ROWS = 16
STAGE_PITCH = 72
TOKEN_TILE = 512
FFN_CHUNK = 256
ATTN_BLOCKS = 16
VMEM_LIMIT = 56 * 1024 * 1024

_BF16 = jnp.bfloat16
_F32 = jnp.float32


class _Layer(NamedTuple):
    stack: jax.Array
    index: int

    @property
    def shape(self):
        return self.stack.shape[1:]


def _const_spec(a):
    if isinstance(a, _Layer):
        zeros = (0,) * len(a.shape)
        return pl.BlockSpec((None,) + a.shape, lambda *_: (a.index,) + zeros, pipeline_mode=pl.Buffered(1))
    return pl.BlockSpec(a.shape, lambda *_: (0,) * a.ndim, pipeline_mode=pl.Buffered(1))


def _arrays(operands):
    return [a.stack if isinstance(a, _Layer) else a for a in operands]


def _params(semantics):
    return pltpu.CompilerParams(dimension_semantics=semantics, vmem_limit_bytes=VMEM_LIMIT)


def _rms(x, g):
    ms = jnp.mean(x * x, axis=-1, keepdims=True)
    return x * lax.rsqrt(ms + NORM_EPS) * g


def _silu(x):
    return x * jax.nn.sigmoid(x)


def _dot(a, b):
    return jnp.dot(a, b, preferred_element_type=_F32)


def _dot_nt(a, b):
    return lax.dot_general(a, b, (((1,), (1,)), ((), ())), preferred_element_type=_F32)


def _glu_rows_t(h, g_ref, wint_ref, bint_ref):
    d = h.shape[1]
    u = _rms(h, g_ref[...]).astype(_BF16)
    a2t = _dot_nt(wint_ref[...], u) + bint_ref[...]
    return (a2t[:d] * jax.nn.sigmoid(a2t[d:])).astype(_BF16)


def _glu_t_kernel(h_ref, g_ref, wint_ref, bint_ref, at_ref):
    at_ref[...] = _glu_rows_t(h_ref[...], g_ref, wint_ref, bint_ref)


def _glu_t(h, glu_w):
    n, d = h.shape
    tm = min(2 * TOKEN_TILE, n)
    return pl.pallas_call(
        _glu_t_kernel,
        grid=(n // tm,),
        in_specs=[pl.BlockSpec((tm, d), lambda i: (i, 0))] + [_const_spec(a) for a in glu_w],
        out_specs=pl.BlockSpec((d, tm), lambda i: (0, i)),
        out_shape=jax.ShapeDtypeStruct((d, n), _BF16),
        compiler_params=_params(("parallel",)),
        name="glu_t",
    )(h, *_arrays(glu_w))


def _dwconv_t_kernel(at_ref, extra_ref, taps_ref, ct_ref, cextra_ref, stage_ref, *, blocks_per_seq):
    cb, nb, _ = at_ref.shape
    u_idx = lax.broadcasted_iota(jnp.int32, (BLOCK, BLOCK), 0)
    t_idx = lax.broadcasted_iota(jnp.int32, (BLOCK, BLOCK), 1)
    in_block = t_idx >= u_idx
    seq_start = (lax.broadcasted_iota(jnp.int32, (nb, BLOCK), 0) & (blocks_per_seq - 1)) == 0
    for ci in range(cb):
        circ = pltpu.roll(jnp.broadcast_to(taps_ref[ci:ci + 1, :], (BLOCK, BLOCK)), 0, 1, stride=1, stride_axis=0)
        w = jnp.concatenate([jnp.where(in_block, circ, 0.0), jnp.where(in_block, 0.0, circ)], axis=1).astype(_BF16)
        x = jnp.concatenate([at_ref[ci], extra_ref[ci]], axis=0)
        y = _dot(x, w)
        from_prev = jnp.where(seq_start, y[nb:nb + 1, BLOCK:], pltpu.roll(y[:nb, BLOCK:], 1, axis=0))
        stage_ref[pl.ds(ci, nb, stride=STAGE_PITCH), :] = y[:nb, :BLOCK] + from_prev
        cextra_ref[ci] = y[nb:, :BLOCK]
    for blk in range(nb):
        ct_ref[:, blk * BLOCK:(blk + 1) * BLOCK] = stage_ref[blk * STAGE_PITCH:blk * STAGE_PITCH + cb, :]


def _dwconv_t(at, extra, taps, *, blocks_per_seq):
    d, n = at.shape
    nb = n // BLOCK
    cb = CONV_CHANNELS
    chan = lambda i: (i, 0, 0)
    return pl.pallas_call(
        functools.partial(_dwconv_t_kernel, blocks_per_seq=blocks_per_seq),
        grid=(d // cb,),
        in_specs=[pl.BlockSpec((cb, nb, BLOCK), chan), pl.BlockSpec((cb, EXTRA_ROWS, BLOCK), chan),
                  pl.BlockSpec((cb, BLOCK), lambda i: (i, 0))],
        out_specs=[pl.BlockSpec((cb, n), lambda i: (i, 0)), pl.BlockSpec((cb, EXTRA_ROWS, BLOCK), chan)],
        out_shape=[jax.ShapeDtypeStruct((d, n), _F32), jax.ShapeDtypeStruct((d, EXTRA_ROWS, BLOCK), _F32)],
        scratch_shapes=[pltpu.VMEM((nb * STAGE_PITCH, BLOCK), _F32)],
        compiler_params=_params(("parallel",)),
        name="dwconv_t",
    )(at.reshape(d, nb, BLOCK), extra, taps)


def _head_pair_rms(x, gain_pair, scale):
    low = lax.broadcasted_iota(jnp.int32, x.shape, 1) < HEAD_DIM
    sq = x * x
    ms_lo = jnp.sum(jnp.where(low, sq, 0.0), axis=-1, keepdims=True) * (1.0 / HEAD_DIM)
    ms_hi = jnp.sum(jnp.where(low, 0.0, sq), axis=-1, keepdims=True) * (1.0 / HEAD_DIM)
    inv = jnp.where(low, lax.rsqrt(ms_lo + NORM_EPS), lax.rsqrt(ms_hi + NORM_EPS))
    return x * inv * gain_pair * scale


def _ffn_kernel(*refs, pre_conv, pre_oproj, post_q, post_kv, post_glu):
    refs = list(refs)
    h_ref = refs.pop(0)
    h = h_ref[...]
    if pre_conv:
        ct_ref, lng_ref, lnb_ref, woutt_ref, bout_ref = (refs.pop(0) for _ in range(5))
        c = ct_ref[...]
        mu = jnp.mean(c, axis=0, keepdims=True)
        cen = c - mu
        var = jnp.mean(cen * cen, axis=0, keepdims=True)
        y = _silu(cen * lax.rsqrt(var + NORM_EPS) * lng_ref[...] + lnb_ref[...]).astype(_BF16)
        h = h + lax.dot_general(y, woutt_ref[...], (((0,), (1,)), ((), ())), preferred_element_type=_F32) + bout_ref[...]
    if pre_oproj:
        a_ref, wo_ref = refs.pop(0), refs.pop(0)
        h = h + _dot(a_ref[...], wo_ref[...])
    g_ref, wg_ref, wu_ref, wd_ref = (refs.pop(0) for _ in range(4))
    if post_q:
        gq_ref, wq_ref, qg_ref = (refs.pop(0) for _ in range(3))
    if post_kv:
        gkv_ref, wk_ref, wvt_ref, kg_ref = (refs.pop(0) for _ in range(4))
    if post_glu:
        gglu_ref, wint_ref, bint_ref = (refs.pop(0) for _ in range(3))
    o_ref = refs.pop(0)

    u = _rms(h, g_ref[...]).astype(_BF16)
    acc = None
    for k in range(wg_ref.shape[1] // FFN_CHUNK):
        fc = slice(k * FFN_CHUNK, (k + 1) * FFN_CHUNK)
        z = (_silu(_dot(u, wg_ref[:, fc])) * _dot(u, wu_ref[:, fc])).astype(_BF16)
        part = _dot(z, wd_ref[fc, :])
        acc = part if acc is None else acc + part
    h = h + acc
    o_ref[...] = h

    if post_q:
        q_ref = refs.pop(0)
        q = _dot(_rms(h, gq_ref[...]).astype(_BF16), wq_ref[...])
        for c in range(q_ref.shape[1] // LANES):
            cl = slice(c * LANES, (c + 1) * LANES)
            q_ref[:, cl] = _head_pair_rms(q[:, cl], qg_ref[...], HEAD_DIM ** -0.5).astype(_BF16)
    if post_kv:
        k_ref, vt_ref = refs.pop(0), refs.pop(0)
        ukv = _rms(h, gkv_ref[...]).astype(_BF16)
        k = _dot(ukv, wk_ref[...])
        for c in range(k_ref.shape[1] // LANES):
            cl = slice(c * LANES, (c + 1) * LANES)
            k_ref[:, cl] = _rms(k[:, cl], kg_ref[...]).astype(_BF16)
        vt_ref[...] = _dot_nt(wvt_ref[...], ukv).astype(_BF16)
    if post_glu:
        at_ref = refs.pop(0)
        at_ref[...] = _glu_rows_t(h, gglu_ref, wint_ref, bint_ref)


def _ffn(h, ffn_w, *, conv=None, oproj=None, q_w=None, kv_w=None, glu_w=None):
    n, d = h.shape
    tm = min(TOKEN_TILE if conv is not None or kv_w is not None else 2 * TOKEN_TILE, n)
    row = lambda i: (i, 0)
    col = lambda i: (0, i)
    args, in_specs = [h], [pl.BlockSpec((tm, d), row)]
    if conv is not None:
        args += list(conv)
        in_specs += [pl.BlockSpec((d, tm), col)] + [_const_spec(a) for a in conv[1:]]
    if oproj is not None:
        args += list(oproj)
        in_specs += [pl.BlockSpec((tm, oproj[0].shape[1]), row), _const_spec(oproj[1])]
    args += list(ffn_w)
    in_specs += [_const_spec(a) for a in ffn_w]
    out_specs = [pl.BlockSpec((tm, d), row)]
    out_shape = [jax.ShapeDtypeStruct((n, d), _F32)]
    if q_w is not None:
        qd = q_w[1].shape[1]
        args += list(q_w)
        in_specs += [_const_spec(a) for a in q_w]
        out_specs.append(pl.BlockSpec((tm, qd), row))
        out_shape.append(jax.ShapeDtypeStruct((n, qd), _BF16))
    if kv_w is not None:
        kd = kv_w[1].shape[1]
        args += list(kv_w)
        in_specs += [_const_spec(a) for a in kv_w]
        out_specs += [pl.BlockSpec((tm, kd), row), pl.BlockSpec((kd, tm), col)]
        out_shape += [jax.ShapeDtypeStruct((n, kd), _BF16), jax.ShapeDtypeStruct((kd, n), _BF16)]
    if glu_w is not None:
        args += list(glu_w)
        in_specs += [_const_spec(a) for a in glu_w]
        out_specs.append(pl.BlockSpec((d, tm), col))
        out_shape.append(jax.ShapeDtypeStruct((d, n), _BF16))
    return pl.pallas_call(
        functools.partial(_ffn_kernel, pre_conv=conv is not None, pre_oproj=oproj is not None,
                          post_q=q_w is not None, post_kv=kv_w is not None, post_glu=glu_w is not None),
        grid=(n // tm,),
        in_specs=in_specs, out_specs=out_specs, out_shape=out_shape,
        compiler_params=_params(("parallel",)),
        name="ffn",
    )(*_arrays(args))


def _block_scores(q, k_prev, k_cur, k_meta):
    low = lax.broadcasted_iota(jnp.int32, (BLOCK, LANES), 1) < HEAD_DIM
    zero_bf = jnp.zeros((BLOCK, LANES), _BF16)
    scores = []
    for g in range(N_KV_HEADS):
        gl = slice(g * LANES, (g + 1) * LANES)
        parts = []
        for p in range(Q_PER_KV // 2):
            qp = q[:, (2 * g + p) * LANES:(2 * g + p + 1) * LANES]
            parts += [jnp.where(low, qp, zero_bf), jnp.where(low, zero_bf, qp)]
        qs = jnp.concatenate(parts, axis=0)
        keys = jnp.concatenate([k_prev[:, gl], k_cur[:, gl], k_meta[:, gl]], axis=0)
        st = _dot_nt(keys, qs)
        scores.append((st[:2 * BLOCK], st[2 * BLOCK:]))
    return scores


def _block_finish(sink_ref, scores, vt_prev, vt_cur, vt_meta, prev_bias):
    nq = Q_PER_KV * BLOCK
    row_low = lax.broadcasted_iota(jnp.int32, (LANES, BLOCK), 0) < HEAD_DIM
    c = lax.broadcasted_iota(jnp.int32, (BLOCK, nq), 0)
    r = lax.broadcasted_iota(jnp.int32, (BLOCK, nq), 1) & (BLOCK - 1)
    use_cur = c <= r
    meta_pad = jnp.zeros((BLOCK - N_META, nq), _BF16)
    out = []
    for g in range(N_KV_HEADS):
        gl = slice(g * LANES, (g + 1) * LANES)
        st, sm = scores[g]
        s = jnp.where(use_cur, st[BLOCK:], st[:BLOCK] + prev_bias)
        sink = jnp.concatenate(
            [jnp.full((1, BLOCK), sink_ref[g * Q_PER_KV + hh], _F32) for hh in range(Q_PER_KV)], axis=1)
        mx = jnp.maximum(jnp.maximum(jnp.max(s, axis=0, keepdims=True),
                                     jnp.max(sm, axis=0, keepdims=True)), sink)
        pe = jnp.exp(s - mx)
        pme = jnp.exp(sm - mx)
        denom = jnp.sum(pe, axis=0, keepdims=True) + jnp.sum(pme, axis=0, keepdims=True) + jnp.exp(sink - mx)
        p_cur = jnp.where(use_cur, pe, 0.0)
        pt = jnp.concatenate([(pe - p_cur).astype(_BF16), p_cur.astype(_BF16), pme.astype(_BF16), meta_pad],
                             axis=0)
        vt = jnp.concatenate([vt_prev[gl, :], vt_cur[gl, :], vt_meta[gl, :]], axis=1)
        ot = _dot(vt, pt) / denom
        for p in range(Q_PER_KV // 2):
            pair_t = jnp.where(row_low, ot[:, 2 * p * BLOCK:(2 * p + 1) * BLOCK],
                               ot[:, (2 * p + 1) * BLOCK:(2 * p + 2) * BLOCK])
            out.append(pair_t.T.astype(_BF16))
    return jnp.concatenate(out, axis=1)


def _attn_kernel(sink_ref, q_ref, kp_ref, kc_ref, km_ref, vtp_ref, vtc_ref, vtm_ref, o_ref):
    first_bias = jnp.where(pl.program_id(1) > 0, 0.0, NEG_INF)
    rows = [slice(sub * BLOCK, (sub + 1) * BLOCK) for sub in range(ATTN_BLOCKS)]

    def scores(sub):
        k_prev = kp_ref[...] if sub == 0 else kc_ref[rows[sub - 1], :]
        return _block_scores(q_ref[rows[sub], :], k_prev, kc_ref[rows[sub], :], km_ref[...])

    pending = scores(0)
    for sub in range(ATTN_BLOCKS):
        ready, pending = pending, (scores(sub + 1) if sub + 1 < ATTN_BLOCKS else None)
        vt_prev = vtp_ref[...] if sub == 0 else vtc_ref[:, rows[sub - 1]]
        o_ref[rows[sub], :] = _block_finish(sink_ref, ready, vt_prev, vtc_ref[:, rows[sub]], vtm_ref[...],
                                            first_bias if sub == 0 else 0.0)


def _attention(sinks, q, k, vt, k_meta, vt_meta, *, batch, n_blocks):
    n, qd = q.shape
    kd = k.shape[1]
    steps = n_blocks // ATTN_BLOCKS
    rows = ATTN_BLOCKS * BLOCK
    cur = lambda b, j: (b * steps + j, 0)
    prev = lambda b, j: (b * n_blocks + jnp.maximum(j * ATTN_BLOCKS - 1, 0), 0)
    cur_t = lambda b, j: (0, b * steps + j)
    prev_t = lambda b, j: (0, b * n_blocks + jnp.maximum(j * ATTN_BLOCKS - 1, 0))
    return pl.pallas_call(
        _attn_kernel,
        grid=(batch, steps),
        in_specs=[
            pl.BlockSpec(memory_space=pltpu.SMEM),
            pl.BlockSpec((rows, qd), cur),
            pl.BlockSpec((BLOCK, kd), prev), pl.BlockSpec((rows, kd), cur), _const_spec(k_meta),
            pl.BlockSpec((kd, BLOCK), prev_t), pl.BlockSpec((kd, rows), cur_t), _const_spec(vt_meta),
        ],
        out_specs=pl.BlockSpec((rows, qd), cur),
        out_shape=jax.ShapeDtypeStruct((n, qd), _BF16),
        compiler_params=_params(("parallel", "parallel")),
        name="swa_attention",
    )(sinks, q, k, k, k_meta, vt, vt, vt_meta)


def _dup_heads(w):
    lead = w.shape[:-1]
    w = w.reshape(lead + (N_KV_HEADS, 1, HEAD_DIM))
    return jnp.broadcast_to(w, lead + (N_KV_HEADS, 2, HEAD_DIM)).reshape(lead + (N_KV_HEADS * 2 * HEAD_DIM,))


def kernel(x, meta_tokens, norm_mix, norm_ffn, conv_w_in, conv_b_in, conv_dw, conv_ln_g, conv_ln_b,
           conv_w_out, conv_b_out, kv_norm, w_kv, k_norm, w_q, q_norm, attn_sinks, w_o,
           ffn_w_gate, ffn_w_up, ffn_w_down):
    b, seq, d = x.shape
    depth = norm_mix.shape[0]
    n_a = conv_w_in.shape[0]
    assert seq % TOKEN_TILE == 0 and TOKEN_TILE % BLOCK == 0 and n_a >= 1 and depth > n_a
    assert CONV_WIDTH - 1 <= BLOCK and N_META <= BLOCK and d % CONV_CHANNELS == 0
    assert (seq // BLOCK) & (seq // BLOCK - 1) == 0 and (seq // BLOCK) % ATTN_BLOCKS == 0

    row = lambda v: v.reshape(1, -1).astype(_F32)
    col = lambda v: v.reshape(-1, 1).astype(_F32)
    kvd = w_kv.shape[1] // 2
    kv_w = (row(kv_norm), _dup_heads(w_kv[:, :kvd]).astype(_BF16), _dup_heads(w_kv[:, kvd:]).T.astype(_BF16),
            jnp.tile(k_norm.astype(_F32), 2).reshape(1, LANES))
    wq_s, wo_s = w_q.astype(_BF16), w_o.astype(_BF16)
    wg_s, wu_s, wd_s = ffn_w_gate.astype(_BF16), ffn_w_up.astype(_BF16), ffn_w_down.astype(_BF16)
    wint_s = jnp.swapaxes(conv_w_in, 1, 2).astype(_BF16)
    woutt_s = jnp.swapaxes(conv_w_out, 1, 2).astype(_BF16)
    q_ws = [(row(norm_mix[n_a + j]), _Layer(wq_s, j), jnp.tile(q_norm[j].astype(_F32), 2).reshape(1, LANES))
            for j in range(depth - n_a)]
    ffn_ws = [(row(norm_ffn[i]), _Layer(wg_s, i), _Layer(wu_s, i), _Layer(wd_s, i)) for i in range(depth)]
    glu_ws = [(row(norm_mix[i]), _Layer(wint_s, i), col(conv_b_in[i])) for i in range(n_a)]

    hm = jnp.pad(meta_tokens.astype(_F32), ((0, BLOCK - N_META), (0, 0)))
    h = x.reshape(b * seq, d)
    at_m, at = _glu_t(hm, glu_ws[0]), _glu_t(h, glu_ws[0])
    k_meta = vt_meta = k_sh = vt_sh = q = None
    for i in range(n_a):
        taps = jnp.pad(conv_dw[i].astype(_F32)[::-1].T, ((0, 0), (0, BLOCK - CONV_WIDTH)))
        conv_tail = (col(conv_ln_g[i]), col(conv_ln_b[i]), _Layer(woutt_s, i), row(conv_b_out[i]))
        history = jnp.pad(at_m[:, :N_META], ((0, 0), (BLOCK - N_META, 0)))
        extra = jnp.pad(jnp.stack([history, at_m], axis=1), ((0, 0), (0, EXTRA_ROWS - 2), (0, 0)))
        ct, cextra = _dwconv_t(at, extra, taps, blocks_per_seq=seq // BLOCK)
        conv_m = (cextra[:, 1, :],) + conv_tail
        conv_r = (ct,) + conv_tail
        if i == n_a - 1:
            _, _, k_meta, vt_meta = _ffn(hm, ffn_ws[i], conv=conv_m, q_w=q_ws[0], kv_w=kv_w)
            h, q, k_sh, vt_sh = _ffn(h, ffn_ws[i], conv=conv_r, q_w=q_ws[0], kv_w=kv_w)
        else:
            hm, at_m = _ffn(hm, ffn_ws[i], conv=conv_m, glu_w=glu_ws[i + 1])
            h, at = _ffn(h, ffn_ws[i], conv=conv_r, glu_w=glu_ws[i + 1])
    k_meta = k_meta[:N_META]
    vt_meta = jnp.where(lax.broadcasted_iota(jnp.int32, vt_meta.shape, 1) < N_META, vt_meta, 0)
    for j in range(depth - n_a):
        layer = n_a + j
        a = _attention(attn_sinks[j].astype(_F32), q, k_sh, vt_sh, k_meta, vt_meta, batch=b, n_blocks=seq // BLOCK)
        oproj = (a, _Layer(wo_s, j))
        if layer + 1 < depth:
            h, q = _ffn(h, ffn_ws[layer], oproj=oproj, q_w=q_ws[j + 1])
        else:
            h, = _ffn(h, ffn_ws[layer], oproj=oproj)
    return h.reshape(b, seq, d)
```

```python
import functools
from typing import NamedTuple

import jax
import jax.numpy as jnp
from jax import lax
from jax.experimental import pallas as pl
from jax.experimental.pallas import tpu as pltpu

N_META = 16
CONV_WIDTH = 31
HEAD_DIM = 64
N_KV_HEADS = 4
Q_PER_KV = 4
BLOCK = 128
NORM_EPS = 1e-6
NEG_INF = -1e30

LANES = 128
CONV_CHANNELS = 64
EXTRA_ROWS = 16
STAGE_PITCH = 72
TOKEN_TILE = 512
FFN_CHUNK = 256
ATTN_BLOCKS = 16
VMEM_LIMIT = 56 * 1024 * 1024

_BF16 = jnp.bfloat16
_F32 = jnp.float32


class _Layer(NamedTuple):
    stack: jax.Array
    index: int

    @property
    def shape(self):
        return self.stack.shape[1:]


def _const_spec(a):
    if isinstance(a, _Layer):
        zeros = (0,) * len(a.shape)
        return pl.BlockSpec((None,) + a.shape, lambda *_: (a.index,) + zeros, pipeline_mode=pl.Buffered(1))
    return pl.BlockSpec(a.shape, lambda *_: (0,) * a.ndim, pipeline_mode=pl.Buffered(1))


def _arrays(operands):
    return [a.stack if isinstance(a, _Layer) else a for a in operands]


def _params(semantics):
    return pltpu.CompilerParams(dimension_semantics=semantics, vmem_limit_bytes=VMEM_LIMIT)


def _rms(x, g):
    ms = jnp.mean(x * x, axis=-1, keepdims=True)
    return x * lax.rsqrt(ms + NORM_EPS) * g


def _silu(x):
    return x * jax.nn.sigmoid(x)


def _dot(a, b):
    return jnp.dot(a, b, preferred_element_type=_F32)


def _dot_nt(a, b):
    return lax.dot_general(a, b, (((1,), (1,)), ((), ())), preferred_element_type=_F32)


def _glu_rows_t(h, g_ref, wint_ref, bint_ref):
    d = h.shape[1]
    u = _rms(h, g_ref[...]).astype(_BF16)
    a2t = _dot_nt(wint_ref[...], u) + bint_ref[...]
    return (a2t[:d] * jax.nn.sigmoid(a2t[d:])).astype(_BF16)


def _glu_t_kernel(h_ref, g_ref, wint_ref, bint_ref, at_ref):
    at_ref[...] = _glu_rows_t(h_ref[...], g_ref, wint_ref, bint_ref)


def _glu_t(h, glu_w):
    n, d = h.shape
    tm = min(2 * TOKEN_TILE, n)
    return pl.pallas_call(
        _glu_t_kernel,
        grid=(n // tm,),
        in_specs=[pl.BlockSpec((tm, d), lambda i: (i, 0))] + [_const_spec(a) for a in glu_w],
        out_specs=pl.BlockSpec((d, tm), lambda i: (0, i)),
        out_shape=jax.ShapeDtypeStruct((d, n), _BF16),
        compiler_params=_params(("parallel",)),
        name="glu_t",
    )(h, *_arrays(glu_w))


def _dwconv_t_kernel(at_ref, extra_ref, taps_ref, ct_ref, cextra_ref, stage_ref, *, blocks_per_seq):
    cb, nb, _ = at_ref.shape
    u_idx = lax.broadcasted_iota(jnp.int32, (BLOCK, BLOCK), 0)
    t_idx = lax.broadcasted_iota(jnp.int32, (BLOCK, BLOCK), 1)
    in_block = t_idx >= u_idx
    seq_start = (lax.broadcasted_iota(jnp.int32, (nb, BLOCK), 0) & (blocks_per_seq - 1)) == 0
    for ci in range(cb):
        circ = pltpu.roll(jnp.broadcast_to(taps_ref[ci:ci + 1, :], (BLOCK, BLOCK)), 0, 1, stride=1, stride_axis=0)
        w = jnp.concatenate([jnp.where(in_block, circ, 0.0), jnp.where(in_block, 0.0, circ)], axis=1).astype(_BF16)
        x = jnp.concatenate([at_ref[ci], extra_ref[ci]], axis=0)
        y = _dot(x, w)
        from_prev = jnp.where(seq_start, y[nb:nb + 1, BLOCK:], pltpu.roll(y[:nb, BLOCK:], 1, axis=0))
        stage_ref[pl.ds(ci, nb, stride=STAGE_PITCH), :] = y[:nb, :BLOCK] + from_prev
        cextra_ref[ci] = y[nb:, :BLOCK]
    for blk in range(nb):
        tile = stage_ref[blk * STAGE_PITCH:blk * STAGE_PITCH + cb, :]
        ct_ref[:, blk * BLOCK:(blk + 1) * BLOCK] = tile.astype(ct_ref.dtype)


def _dwconv_t(at, extra, taps, *, blocks_per_seq):
    d, n = at.shape
    nb = n // BLOCK
    cb = CONV_CHANNELS
    chan = lambda i: (i, 0, 0)
    return pl.pallas_call(
        functools.partial(_dwconv_t_kernel, blocks_per_seq=blocks_per_seq),
        grid=(d // cb,),
        in_specs=[pl.BlockSpec((cb, nb, BLOCK), chan), pl.BlockSpec((cb, EXTRA_ROWS, BLOCK), chan),
                  pl.BlockSpec((cb, BLOCK), lambda i: (i, 0))],
        out_specs=[pl.BlockSpec((cb, n), lambda i: (i, 0)), pl.BlockSpec((cb, EXTRA_ROWS, BLOCK), chan)],
        out_shape=[jax.ShapeDtypeStruct((d, n), _BF16), jax.ShapeDtypeStruct((d, EXTRA_ROWS, BLOCK), _F32)],
        scratch_shapes=[pltpu.VMEM((nb * STAGE_PITCH, BLOCK), _F32)],
        compiler_params=_params(("parallel",)),
        name="dwconv_t",
    )(at.reshape(d, nb, BLOCK), extra, taps)


def _head_pair_rms(x, gain_pair, scale):
    low = lax.broadcasted_iota(jnp.int32, x.shape, 1) < HEAD_DIM
    sq = x * x
    ms_lo = jnp.sum(jnp.where(low, sq, 0.0), axis=-1, keepdims=True) * (1.0 / HEAD_DIM)
    ms_hi = jnp.sum(jnp.where(low, 0.0, sq), axis=-1, keepdims=True) * (1.0 / HEAD_DIM)
    inv = jnp.where(low, lax.rsqrt(ms_lo + NORM_EPS), lax.rsqrt(ms_hi + NORM_EPS))
    return x * inv * gain_pair * scale


def _ffn_kernel(*refs, pre_conv, pre_oproj, post_q, post_kv, post_glu):
    refs = list(refs)
    h_ref = refs.pop(0)
    h = h_ref[...]
    if pre_conv:
        ct_ref, lng_ref, lnb_ref, woutt_ref, bout_ref = (refs.pop(0) for _ in range(5))
        c = ct_ref[...].astype(_F32)
        mu = jnp.mean(c, axis=0, keepdims=True)
        cen = c - mu
        var = jnp.mean(cen * cen, axis=0, keepdims=True)
        y = _silu(cen * lax.rsqrt(var + NORM_EPS) * lng_ref[...] + lnb_ref[...]).astype(_BF16)
        h = h + lax.dot_general(y, woutt_ref[...], (((0,), (1,)), ((), ())), preferred_element_type=_F32) + bout_ref[...]
    if pre_oproj:
        a_ref, wo_ref = refs.pop(0), refs.pop(0)
        h = h + _dot(a_ref[...], wo_ref[...])
    g_ref, wg_ref, wu_ref, wd_ref = (refs.pop(0) for _ in range(4))
    if post_q:
        gq_ref, wq_ref, qg_ref = (refs.pop(0) for _ in range(3))
    if post_kv:
        gkv_ref, wk_ref, wvt_ref, kg_ref = (refs.pop(0) for _ in range(4))
    if post_glu:
        gglu_ref, wint_ref, bint_ref = (refs.pop(0) for _ in range(3))
    o_ref = refs.pop(0)

    u = _rms(h, g_ref[...]).astype(_BF16)
    acc = None
    for k in range(wg_ref.shape[1] // FFN_CHUNK):
        fc = slice(k * FFN_CHUNK, (k + 1) * FFN_CHUNK)
        z = (_silu(_dot(u, wg_ref[:, fc])) * _dot(u, wu_ref[:, fc])).astype(_BF16)
        part = _dot(z, wd_ref[fc, :])
        acc = part if acc is None else acc + part
    h = h + acc
    o_ref[...] = h

    if post_q:
        q_ref = refs.pop(0)
        q = _dot(_rms(h, gq_ref[...]).astype(_BF16), wq_ref[...])
        for c in range(q_ref.shape[1] // LANES):
            cl = slice(c * LANES, (c + 1) * LANES)
            q_ref[:, cl] = _head_pair_rms(q[:, cl], qg_ref[...], HEAD_DIM ** -0.5).astype(_BF16)
    if post_kv:
        k_ref, vt_ref = refs.pop(0), refs.pop(0)
        ukv = _rms(h, gkv_ref[...]).astype(_BF16)
        k = _dot(ukv, wk_ref[...])
        for c in range(k_ref.shape[1] // LANES):
            cl = slice(c * LANES, (c + 1) * LANES)
            k_ref[:, cl] = _rms(k[:, cl], kg_ref[...]).astype(_BF16)
        vt_ref[...] = _dot_nt(wvt_ref[...], ukv).astype(_BF16)
    if post_glu:
        at_ref = refs.pop(0)
        at_ref[...] = _glu_rows_t(h, gglu_ref, wint_ref, bint_ref)


def _ffn(h, ffn_w, *, conv=None, oproj=None, q_w=None, kv_w=None, glu_w=None):
    n, d = h.shape
    tm = min(TOKEN_TILE if conv is not None or kv_w is not None else 2 * TOKEN_TILE, n)
    row = lambda i: (i, 0)
    col = lambda i: (0, i)
    args, in_specs = [h], [pl.BlockSpec((tm, d), row)]
    if conv is not None:
        args += list(conv)
        in_specs += [pl.BlockSpec((d, tm), col)] + [_const_spec(a) for a in conv[1:]]
    if oproj is not None:
        args += list(oproj)
        in_specs += [pl.BlockSpec((tm, oproj[0].shape[1]), row), _const_spec(oproj[1])]
    args += list(ffn_w)
    in_specs += [_const_spec(a) for a in ffn_w]
    out_specs = [pl.BlockSpec((tm, d), row)]
    out_shape = [jax.ShapeDtypeStruct((n, d), _F32)]
    if q_w is not None:
        qd = q_w[1].shape[1]
        args += list(q_w)
        in_specs += [_const_spec(a) for a in q_w]
        out_specs.append(pl.BlockSpec((tm, qd), row))
        out_shape.append(jax.ShapeDtypeStruct((n, qd), _BF16))
    if kv_w is not None:
        kd = kv_w[1].shape[1]
        args += list(kv_w)
        in_specs += [_const_spec(a) for a in kv_w]
        out_specs += [pl.BlockSpec((tm, kd), row), pl.BlockSpec((kd, tm), col)]
        out_shape += [jax.ShapeDtypeStruct((n, kd), _BF16), jax.ShapeDtypeStruct((kd, n), _BF16)]
    if glu_w is not None:
        args += list(glu_w)
        in_specs += [_const_spec(a) for a in glu_w]
        out_specs.append(pl.BlockSpec((d, tm), col))
        out_shape.append(jax.ShapeDtypeStruct((d, n), _BF16))
    return pl.pallas_call(
        functools.partial(_ffn_kernel, pre_conv=conv is not None, pre_oproj=oproj is not None,
                          post_q=q_w is not None, post_kv=kv_w is not None, post_glu=glu_w is not None),
        grid=(n // tm,),
        in_specs=in_specs, out_specs=out_specs, out_shape=out_shape,
        compiler_params=_params(("parallel",)),
        name="ffn",
    )(*_arrays(args))


def _block_scores(q, k_prev, k_cur, k_meta):
    low = lax.broadcasted_iota(jnp.int32, (BLOCK, LANES), 1) < HEAD_DIM
    zero_bf = jnp.zeros((BLOCK, LANES), _BF16)
    scores = []
    for g in range(N_KV_HEADS):
        gl = slice(g * LANES, (g + 1) * LANES)
        parts = []
        for p in range(Q_PER_KV // 2):
            qp = q[:, (2 * g + p) * LANES:(2 * g + p + 1) * LANES]
            parts += [jnp.where(low, qp, zero_bf), jnp.where(low, zero_bf, qp)]
        qs = jnp.concatenate(parts, axis=0)
        keys = jnp.concatenate([k_prev[:, gl], k_cur[:, gl], k_meta[:, gl]], axis=0)
        st = _dot_nt(keys, qs)
        scores.append((st[:2 * BLOCK], st[2 * BLOCK:]))
    return scores


def _block_finish(sink_ref, scores, vt_prev, vt_cur, vt_meta, prev_bias):
    nq = Q_PER_KV * BLOCK
    row_low = lax.broadcasted_iota(jnp.int32, (LANES, BLOCK), 0) < HEAD_DIM
    c = lax.broadcasted_iota(jnp.int32, (BLOCK, nq), 0)
    r = lax.broadcasted_iota(jnp.int32, (BLOCK, nq), 1) & (BLOCK - 1)
    use_cur = c <= r
    meta_pad = jnp.zeros((BLOCK - N_META, nq), _BF16)
    out = []
    for g in range(N_KV_HEADS):
        gl = slice(g * LANES, (g + 1) * LANES)
        st, sm = scores[g]
        s = jnp.where(use_cur, st[BLOCK:], st[:BLOCK] + prev_bias)
        sink = jnp.concatenate(
            [jnp.full((1, BLOCK), sink_ref[g * Q_PER_KV + hh], _F32) for hh in range(Q_PER_KV)], axis=1)
        mx = jnp.maximum(jnp.maximum(jnp.max(s, axis=0, keepdims=True),
                                     jnp.max(sm, axis=0, keepdims=True)), sink)
        pe = jnp.exp(s - mx)
        pme = jnp.exp(sm - mx)
        denom = jnp.sum(pe, axis=0, keepdims=True) + jnp.sum(pme, axis=0, keepdims=True) + jnp.exp(sink - mx)
        p_cur = jnp.where(use_cur, pe, 0.0)
        pt = jnp.concatenate([(pe - p_cur).astype(_BF16), p_cur.astype(_BF16), pme.astype(_BF16), meta_pad],
                             axis=0)
        vt = jnp.concatenate([vt_prev[gl, :], vt_cur[gl, :], vt_meta[gl, :]], axis=1)
        ot = _dot(vt, pt) / denom
        for p in range(Q_PER_KV // 2):
            pair_t = jnp.where(row_low, ot[:, 2 * p * BLOCK:(2 * p + 1) * BLOCK],
                               ot[:, (2 * p + 1) * BLOCK:(2 * p + 2) * BLOCK])
            out.append(pair_t.T.astype(_BF16))
    return jnp.concatenate(out, axis=1)


def _attn_kernel(sink_ref, q_ref, kp_ref, kc_ref, km_ref, vtp_ref, vtc_ref, vtm_ref, o_ref):
    first_bias = jnp.where(pl.program_id(1) > 0, 0.0, NEG_INF)
    rows = [slice(sub * BLOCK, (sub + 1) * BLOCK) for sub in range(ATTN_BLOCKS)]

    def scores(sub):
        k_prev = kp_ref[...] if sub == 0 else kc_ref[rows[sub - 1], :]
        return _block_scores(q_ref[rows[sub], :], k_prev, kc_ref[rows[sub], :], km_ref[...])

    pending = scores(0)
    for sub in range(ATTN_BLOCKS):
        ready, pending = pending, (scores(sub + 1) if sub + 1 < ATTN_BLOCKS else None)
        vt_prev = vtp_ref[...] if sub == 0 else vtc_ref[:, rows[sub - 1]]
        o_ref[rows[sub], :] = _block_finish(sink_ref, ready, vt_prev, vtc_ref[:, rows[sub]], vtm_ref[...],
                                            first_bias if sub == 0 else 0.0)


def _attention(sinks, q, k, vt, k_meta, vt_meta, *, batch, n_blocks):
    n, qd = q.shape
    kd = k.shape[1]
    steps = n_blocks // ATTN_BLOCKS
    rows = ATTN_BLOCKS * BLOCK
    cur = lambda b, j: (b * steps + j, 0)
    prev = lambda b, j: (b * n_blocks + jnp.maximum(j * ATTN_BLOCKS - 1, 0), 0)
    cur_t = lambda b, j: (0, b * steps + j)
    prev_t = lambda b, j: (0, b * n_blocks + jnp.maximum(j * ATTN_BLOCKS - 1, 0))
    return pl.pallas_call(
        _attn_kernel,
        grid=(batch, steps),
        in_specs=[
            pl.BlockSpec(memory_space=pltpu.SMEM),
            pl.BlockSpec((rows, qd), cur),
            pl.BlockSpec((BLOCK, kd), prev), pl.BlockSpec((rows, kd), cur), _const_spec(k_meta),
            pl.BlockSpec((kd, BLOCK), prev_t), pl.BlockSpec((kd, rows), cur_t), _const_spec(vt_meta),
        ],
        out_specs=pl.BlockSpec((rows, qd), cur),
        out_shape=jax.ShapeDtypeStruct((n, qd), _BF16),
        compiler_params=_params(("parallel", "parallel")),
        name="swa_attention",
    )(sinks, q, k, k, k_meta, vt, vt, vt_meta)


def _dup_heads(w):
    lead = w.shape[:-1]
    w = w.reshape(lead + (N_KV_HEADS, 1, HEAD_DIM))
    return jnp.broadcast_to(w, lead + (N_KV_HEADS, 2, HEAD_DIM)).reshape(lead + (N_KV_HEADS * 2 * HEAD_DIM,))


def kernel(x, meta_tokens, norm_mix, norm_ffn, conv_w_in, conv_b_in, conv_dw, conv_ln_g, conv_ln_b,
           conv_w_out, conv_b_out, kv_norm, w_kv, k_norm, w_q, q_norm, attn_sinks, w_o,
           ffn_w_gate, ffn_w_up, ffn_w_down):
    b, seq, d = x.shape
    depth = norm_mix.shape[0]
    n_a = conv_w_in.shape[0]
    assert seq % TOKEN_TILE == 0 and TOKEN_TILE % BLOCK == 0 and n_a >= 1 and depth > n_a
    assert CONV_WIDTH - 1 <= BLOCK and N_META <= BLOCK and d % CONV_CHANNELS == 0
    assert (seq // BLOCK) & (seq // BLOCK - 1) == 0 and (seq // BLOCK) % ATTN_BLOCKS == 0

    row = lambda v: v.reshape(1, -1).astype(_F32)
    col = lambda v: v.reshape(-1, 1).astype(_F32)
    kvd = w_kv.shape[1] // 2
    kv_w = (row(kv_norm), _dup_heads(w_kv[:, :kvd]).astype(_BF16), _dup_heads(w_kv[:, kvd:]).T.astype(_BF16),
            jnp.tile(k_norm.astype(_F32), 2).reshape(1, LANES))
    wq_s, wo_s = w_q.astype(_BF16), w_o.astype(_BF16)
    wg_s, wu_s, wd_s = ffn_w_gate.astype(_BF16), ffn_w_up.astype(_BF16), ffn_w_down.astype(_BF16)
    wint_s = jnp.swapaxes(conv_w_in, 1, 2).astype(_BF16)
    woutt_s = jnp.swapaxes(conv_w_out, 1, 2).astype(_BF16)
    q_ws = [(row(norm_mix[n_a + j]), _Layer(wq_s, j), jnp.tile(q_norm[j].astype(_F32), 2).reshape(1, LANES))
            for j in range(depth - n_a)]
    ffn_ws = [(row(norm_ffn[i]), _Layer(wg_s, i), _Layer(wu_s, i), _Layer(wd_s, i)) for i in range(depth)]
    glu_ws = [(row(norm_mix[i]), _Layer(wint_s, i), col(conv_b_in[i])) for i in range(n_a)]

    hm = jnp.pad(meta_tokens.astype(_F32), ((0, BLOCK - N_META), (0, 0)))
    h = x.reshape(b * seq, d)
    at_m, at = _glu_t(hm, glu_ws[0]), _glu_t(h, glu_ws[0])
    k_meta = vt_meta = k_sh = vt_sh = q = None
    for i in range(n_a):
        taps = jnp.pad(conv_dw[i].astype(_F32)[::-1].T, ((0, 0), (0, BLOCK - CONV_WIDTH)))
        conv_tail = (col(conv_ln_g[i]), col(conv_ln_b[i]), _Layer(woutt_s, i), row(conv_b_out[i]))
        history = jnp.pad(at_m[:, :N_META], ((0, 0), (BLOCK - N_META, 0)))
        extra = jnp.pad(jnp.stack([history, at_m], axis=1), ((0, 0), (0, EXTRA_ROWS - 2), (0, 0)))
        ct, cextra = _dwconv_t(at, extra, taps, blocks_per_seq=seq // BLOCK)
        conv_m = (cextra[:, 1, :],) + conv_tail
        conv_r = (ct,) + conv_tail
        if i == n_a - 1:
            _, _, k_meta, vt_meta = _ffn(hm, ffn_ws[i], conv=conv_m, q_w=q_ws[0], kv_w=kv_w)
            h, q, k_sh, vt_sh = _ffn(h, ffn_ws[i], conv=conv_r, q_w=q_ws[0], kv_w=kv_w)
        else:
            hm, at_m = _ffn(hm, ffn_ws[i], conv=conv_m, glu_w=glu_ws[i + 1])
            h, at = _ffn(h, ffn_ws[i], conv=conv_r, glu_w=glu_ws[i + 1])
    k_meta = k_meta[:N_META]
    vt_meta = jnp.where(lax.broadcasted_iota(jnp.int32, vt_meta.shape, 1) < N_META, vt_meta, 0)
    for j in range(depth - n_a):
        layer = n_a + j
        a = _attention(attn_sinks[j].astype(_F32), q, k_sh, vt_sh, k_meta, vt_meta, batch=b, n_blocks=seq // BLOCK)
        oproj = (a, _Layer(wo_s, j))
        if layer + 1 < depth:
            h, q = _ffn(h, ffn_ws[layer], oproj=oproj, q_w=q_ws[j + 1])
        else:
            h, = _ffn(h, ffn_ws[layer], oproj=oproj)
    return h.reshape(b, seq, d)
```
